```python
import jax
import jax.numpy as jnp
from jax import lax
import numpy as np

D_MODEL = 2048
BATCH = 2
SEQ = 16384
DEPTH = 2

GRID_W = 64
CTX_LEN = 256
N_MOD = 9
NORM_EPS = 1e-6
FFN_DIM = 5632

ATTN_HEADS = 12
ATTN_KV_HEADS = 4
HEAD_DIM = 128
WINDOW = 128
ATTN_BLOCK = 128
ROPE_BASE = 10000.0

S5_DIM = 1024
S5_GROUP = 16
S5_GROUPS = S5_DIM // S5_GROUP
S5_STATE = 64

SSD_DIM = 1536
SSD_HEAD_DIM = 64
SSD_HEADS = SSD_DIM // SSD_HEAD_DIM
SSD_GROUPS = 4
SSD_STATE = 128
SSD_CONV = 5
SSD_CHUNK = 128

N_BRANCH = 3
Q_DIM = ATTN_HEADS * HEAD_DIM
KV_DIM = ATTN_KV_HEADS * HEAD_DIM
SSD_BC = SSD_GROUPS * SSD_STATE
SSD_XBC = SSD_DIM + 2 * SSD_BC
IN_WIDTHS = (Q_DIM, KV_DIM, KV_DIM, S5_DIM, SSD_DIM, SSD_XBC, SSD_HEADS, SSD_HEADS, N_BRANCH * D_MODEL)
IN_DIM = sum(IN_WIDTHS)

kernel_name = 'hybrid_dit_attn_s5_ssd_trunk'


def rms_norm(t, g):
    tf = t.astype(jnp.float32)
    y = tf * lax.rsqrt(jnp.mean(tf * tf, axis=-1, keepdims=True) + NORM_EPS)
    return (y * g.astype(jnp.float32)).astype(t.dtype)


def modulate(t, g, shift, scale):
    return rms_norm(t, g) * (1 + scale) + shift


def swiglu(t, w_gate, w_up, w_down):
    return (jax.nn.silu(t @ w_gate) * (t @ w_up)) @ w_down


def split_proj(p):
    return jnp.split(p, np.cumsum(IN_WIDTHS)[:-1].tolist(), axis=-1)


def axial_rope_tables(n_tok):
    rows = n_tok // GRID_W
    row = jnp.repeat(jnp.arange(rows, dtype=jnp.float32), GRID_W)
    col = jnp.tile(jnp.arange(GRID_W, dtype=jnp.float32), rows)
    axis_dim = HEAD_DIM // 2
    inv_freq = ROPE_BASE ** (-jnp.arange(0, axis_dim, 2, dtype=jnp.float32) / axis_dim)
    ang_r = row[:, None] * inv_freq[None, :]
    ang_c = col[:, None] * inv_freq[None, :]
    return (jnp.cos(ang_r), jnp.sin(ang_r), jnp.cos(ang_c), jnp.sin(ang_c))


def rope_axis(t, cos, sin):
    t1, t2 = jnp.split(t.astype(jnp.float32), 2, axis=-1)
    cos = cos[:, None, :]
    sin = sin[:, None, :]
    return jnp.concatenate([t1 * cos - t2 * sin, t2 * cos + t1 * sin], axis=-1)


def axial_rope(t, tabs):
    cos_r, sin_r, cos_c, sin_c = tabs
    t_row, t_col = jnp.split(t, 2, axis=-1)
    out = jnp.concatenate([rope_axis(t_row, cos_r, sin_r), rope_axis(t_col, cos_c, sin_c)], axis=-1)
    return out.astype(t.dtype)


def window_attention(q, k, v, k_ctx, v_ctx, sink):
    bsz, n_tok = q.shape[0], q.shape[1]
    nb = n_tok // ATTN_BLOCK
    grp = ATTN_HEADS // ATTN_KV_HEADS
    scale = HEAD_DIM ** -0.5
    qb = q.reshape(bsz, nb, ATTN_BLOCK, ATTN_KV_HEADS, grp, HEAD_DIM)

    def band(t):
        tp = jnp.pad(t, ((0, 0), (ATTN_BLOCK, ATTN_BLOCK), (0, 0), (0, 0)))
        tp = tp.reshape(bsz, nb + 2, ATTN_BLOCK, ATTN_KV_HEADS, HEAD_DIM)
        return jnp.concatenate([tp[:, :-2], tp[:, 1:-1], tp[:, 2:]], axis=2)

    kb, vb = band(k), band(v)
    s_loc = jnp.einsum('bnqkgd,bnskd->bnkgqs', qb, kb, preferred_element_type=jnp.float32) * scale
    s_ctx = jnp.einsum('bnqkgd,bskd->bnkgqs', qb, k_ctx, preferred_element_type=jnp.float32) * scale
    q_pos = jnp.arange(nb)[:, None] * ATTN_BLOCK + jnp.arange(ATTN_BLOCK)[None, :]
    k_pos = (jnp.arange(nb)[:, None] - 1) * ATTN_BLOCK + jnp.arange(3 * ATTN_BLOCK)[None, :]
    mask = ((jnp.abs(q_pos[:, :, None] - k_pos[:, None, :]) <= WINDOW)
            & (k_pos[:, None, :] >= 0) & (k_pos[:, None, :] < n_tok))
    s_loc = jnp.where(mask[None, :, None, None], s_loc, -jnp.inf)
    sink_l = sink.astype(jnp.float32).reshape(ATTN_KV_HEADS, grp)[None, None, :, :, None, None]
    m = jnp.maximum(jnp.maximum(s_loc.max(-1, keepdims=True), s_ctx.max(-1, keepdims=True)), sink_l)
    p_loc = jnp.exp(s_loc - m)
    p_ctx = jnp.exp(s_ctx - m)
    denom = p_loc.sum(-1, keepdims=True) + p_ctx.sum(-1, keepdims=True) + jnp.exp(sink_l - m)
    o = (jnp.einsum('bnkgqs,bnskd->bnqkgd', p_loc, vb)
         + jnp.einsum('bnkgqs,bskd->bnqkgd', p_ctx, v_ctx))
    o = o * jnp.transpose(1.0 / denom, (0, 1, 4, 2, 3, 5))
    return o.reshape(bsz, n_tok, Q_DIM).astype(q.dtype)


def context_attention(q, k, v, sink):
    bsz, n_ctx = q.shape[0], q.shape[1]
    grp = ATTN_HEADS // ATTN_KV_HEADS
    qg = q.reshape(bsz, n_ctx, ATTN_KV_HEADS, grp, HEAD_DIM)
    s = jnp.einsum('bqkgd,bskd->bkgqs', qg, k, preferred_element_type=jnp.float32) * HEAD_DIM ** -0.5
    sink_l = jnp.broadcast_to(sink.astype(jnp.float32).reshape(ATTN_KV_HEADS, grp)[None, :, :, None, None],
                              s.shape[:-1] + (1,))
    p = jax.nn.softmax(jnp.concatenate([s, sink_l], axis=-1), axis=-1)[..., :-1]
    o = jnp.einsum('bkgqs,bskd->bqkgd', p, v)
    return o.reshape(bsz, n_ctx, Q_DIM).astype(q.dtype)


def attention_branch(q, k, v, qc, kc, vc, qk_g, sink, tabs, with_ctx_out):
    def heads(t, n):
        return t.reshape(t.shape[0], t.shape[1], n, HEAD_DIM)
    q = axial_rope(rms_norm(heads(q, ATTN_HEADS), qk_g[0]), tabs)
    k = axial_rope(rms_norm(heads(k, ATTN_KV_HEADS), qk_g[1]), tabs)
    v = heads(v, ATTN_KV_HEADS)
    kc = rms_norm(heads(kc, ATTN_KV_HEADS), qk_g[1])
    vc = heads(vc, ATTN_KV_HEADS)
    o = window_attention(q, k, v, kc, vc, sink)
    oc = None
    if with_ctx_out:
        oc = context_attention(rms_norm(heads(qc, ATTN_HEADS), qk_g[0]), kc, vc, sink)
    return o, oc


def s5_discretize(a_re, a_im, log_step):
    lam = lax.complex(a_re.astype(jnp.float32), a_im.astype(jnp.float32))
    step = jnp.exp(log_step.astype(jnp.float32))[:, None]
    lam_bar = jnp.exp(lam * step)
    return lam_bar, (lam_bar - 1.0) / lam


def s5_scan(lam_bar, bu, h0, reverse):
    if h0 is not None:
        bu = bu.at[:, -1 if reverse else 0].add(lam_bar * h0)
    a = jnp.broadcast_to(lam_bar, bu.shape)

    def combine(e1, e2):
        a1, b1 = e1
        a2, b2 = e2
        return a2 * a1, a2 * b1 + b2

    _, h = lax.associative_scan(combine, (a, bu), axis=1, reverse=reverse)
    return h


def s5_branch(u, uc, a_re, a_im, log_step, b_re, b_im, c_re, c_im, d_skip, w_glu, with_ctx_out):
    lam_f, zoh_f = s5_discretize(a_re[0], a_im[0], log_step[0])
    lam_b, zoh_b = s5_discretize(a_re[1], a_im[1], log_step[1])
    b_re = b_re.astype(jnp.float32)
    b_im = b_im.astype(jnp.float32)
    c_re = c_re.astype(jnp.float32)
    c_im = c_im.astype(jnp.float32)

    def drive(t):
        tg = t.astype(jnp.float32).reshape(t.shape[0], t.shape[1], S5_GROUPS, S5_GROUP)
        return lax.complex(jnp.einsum('blgs,gps->blgp', tg, b_re), jnp.einsum('blgs,gps->blgp', tg, b_im))

    def readout(h, t):
        y = jnp.einsum('blgp,gsp->blgs', h.real, c_re) - jnp.einsum('blgp,gsp->blgs', h.imag, c_im)
        y = y.reshape(t.shape) + d_skip.astype(jnp.float32) * t.astype(jnp.float32)
        g = jax.nn.gelu(y)
        return (g * jax.nn.sigmoid(g @ w_glu.astype(jnp.float32))).astype(t.dtype)

    bu_c = drive(uc)
    h_cf = s5_scan(lam_f, zoh_f * bu_c, None, False)
    h_cb = s5_scan(lam_b, zoh_b * bu_c, None, True)
    bu_l = drive(u)
    h = (s5_scan(lam_f, zoh_f * bu_l, h_cf[:, -1], False)
         + s5_scan(lam_b, zoh_b * bu_l, h_cb[:, 0], True))
    y = readout(h, u)
    yc = readout(h_cf + h_cb, uc) if with_ctx_out else None
    return y, yc


def dwconv_centred(t, w, b):
    n_ch = t.shape[-1]
    y = lax.conv_general_dilated(t, w.astype(t.dtype)[:, None, :], window_strides=(1,),
                                 padding=((SSD_CONV // 2, SSD_CONV // 2),),
                                 dimension_numbers=('NWC', 'WIO', 'NWC'), feature_group_count=n_ch)
    return y + b.astype(t.dtype)


def segsum(a):
    n = a.shape[-1]
    cs = jnp.cumsum(a, axis=-1)
    seg = cs[..., :, None] - cs[..., None, :]
    return jnp.where(jnp.tril(jnp.ones((n, n), dtype=bool)), seg, -jnp.inf)


def ssd_scan(x, dt, a, b, c, h0):
    bsz, n_tok = x.shape[0], x.shape[1]
    nc = n_tok // SSD_CHUNK
    rep = SSD_HEADS // SSD_GROUPS
    xd = (x * dt[..., None]).reshape(bsz, nc, SSD_CHUNK, SSD_GROUPS, rep, SSD_HEAD_DIM)
    ad = jnp.moveaxis((dt * a).reshape(bsz, nc, SSD_CHUNK, SSD_GROUPS, rep), 2, -1)
    bc = b.reshape(bsz, nc, SSD_CHUNK, SSD_GROUPS, SSD_STATE)
    cc = c.reshape(bsz, nc, SSD_CHUNK, SSD_GROUPS, SSD_STATE)
    a_cs = jnp.cumsum(ad, axis=-1)
    decay_in = jnp.exp(segsum(ad))
    cb = jnp.einsum('bclgn,bcsgn->bcgls', cc, bc)
    y_diag = jnp.einsum('bcgls,bcgrls,bcsgrp->bclgrp', cb, decay_in, xd)
    decay_states = jnp.exp(a_cs[..., -1:] - a_cs)
    states = jnp.einsum('bclgn,bcgrl,bclgrp->bcgrpn', bc, decay_states, xd)
    if h0 is None:
        h_init = jnp.zeros((bsz, SSD_GROUPS, rep, SSD_HEAD_DIM, SSD_STATE), jnp.float32)
    else:
        h_init = h0.reshape(bsz, SSD_GROUPS, rep, SSD_HEAD_DIM, SSD_STATE)

    def step(h, inp):
        dec, s = inp
        return h * dec[..., None, None] + s, h

    h_final, h_in = lax.scan(step, h_init, (jnp.moveaxis(jnp.exp(a_cs[..., -1]), 1, 0), jnp.moveaxis(states, 1, 0)))
    h_in = jnp.moveaxis(h_in, 0, 1)
    y_off = jnp.einsum('bclgn,bcgrpn,bcgrl->bclgrp', cc, h_in, jnp.exp(a_cs))
    y = (y_diag + y_off).reshape(bsz, n_tok, SSD_HEADS, SSD_HEAD_DIM)
    return y, h_final.reshape(bsz, SSD_HEADS, SSD_HEAD_DIM, SSD_STATE)


def ssd_branch(z, xbc, dtf, dtb, zc, xbcc, dtfc, dtbc, conv_w, conv_b, dt_bias, a_log, d_skip, norm_g, with_ctx_out):
    def conv_split(t):
        t = jax.nn.silu(dwconv_centred(t, conv_w, conv_b)).astype(jnp.float32)
        xs, bs, cs = jnp.split(t, [SSD_DIM, SSD_DIM + SSD_BC], axis=-1)
        bsz, n_tok = t.shape[0], t.shape[1]
        return (xs.reshape(bsz, n_tok, SSD_HEADS, SSD_HEAD_DIM),
                bs.reshape(bsz, n_tok, SSD_GROUPS, SSD_STATE),
                cs.reshape(bsz, n_tok, SSD_GROUPS, SSD_STATE))

    def dt_of(raw, k):
        return jax.nn.softplus(raw.astype(jnp.float32) + dt_bias[k].astype(jnp.float32))

    def flip(t):
        return jnp.flip(t, axis=1)

    def finish(y, xs, zz):
        y = y + d_skip.astype(jnp.float32)[:, None] * xs
        bsz, n_tok = y.shape[0], y.shape[1]
        y = y.reshape(bsz, n_tok, SSD_DIM) * jax.nn.silu(zz.astype(jnp.float32))
        yg = y.reshape(bsz, n_tok, SSD_GROUPS, SSD_DIM // SSD_GROUPS)
        yg = yg * lax.rsqrt(jnp.mean(yg * yg, axis=-1, keepdims=True) + NORM_EPS)
        return (yg.reshape(bsz, n_tok, SSD_DIM) * norm_g.astype(jnp.float32)).astype(zz.dtype)

    a_f = -jnp.exp(a_log[0].astype(jnp.float32))
    a_b = -jnp.exp(a_log[1].astype(jnp.float32))
    xc_, bc_, cc_ = conv_split(xbcc)
    y_cf, h_cf = ssd_scan(xc_, dt_of(dtfc, 0), a_f, bc_, cc_, None)
    y_cb, h_cb = ssd_scan(flip(xc_), flip(dt_of(dtbc, 1)), a_b, flip(bc_), flip(cc_), None)
    xl, bl, cl = conv_split(xbc)
    y_f, _ = ssd_scan(xl, dt_of(dtf, 0), a_f, bl, cl, h_cf)
    y_b, _ = ssd_scan(flip(xl), flip(dt_of(dtb, 1)), a_b, flip(bl), flip(cl), h_cb)
    y = finish(y_f + flip(y_b), xl, z)
    yc = finish(y_cf + flip(y_cb), xc_, zc) if with_ctx_out else None
    return y, yc


def token_mixers(u, uc, tabs, with_ctx_out, w_in, qk_g, sink, w_attn_o,
                 s5_a_re, s5_a_im, s5_log_step, s5_b_re, s5_b_im, s5_c_re, s5_c_im, s5_d, s5_w_glu, w_s5_o,
                 ssd_conv_w, ssd_conv_b, ssd_dt_bias, ssd_a_log, ssd_d, ssd_norm_g, w_ssd_o, w_out):
    q, k, v, s5u, z, xbc, dtf, dtb, gates = split_proj(u @ w_in)
    qc, kc, vc, s5uc, zc, xbcc, dtfc, dtbc, gates_c = split_proj(uc @ w_in)
    o_attn, oc_attn = attention_branch(q, k, v, qc, kc, vc, qk_g, sink, tabs, with_ctx_out)
    o_s5, oc_s5 = s5_branch(s5u, s5uc, s5_a_re, s5_a_im, s5_log_step, s5_b_re, s5_b_im,
                            s5_c_re, s5_c_im, s5_d, s5_w_glu, with_ctx_out)
    o_ssd, oc_ssd = ssd_branch(z, xbc, dtf, dtb, zc, xbcc, dtfc, dtbc, ssd_conv_w, ssd_conv_b,
                               ssd_dt_bias, ssd_a_log, ssd_d, ssd_norm_g, with_ctx_out)

    def merge(oa, os5, ossd, gl):
        g = jax.nn.sigmoid(gl.astype(jnp.float32)).reshape(gl.shape[:-1] + (N_BRANCH, D_MODEL)).astype(oa.dtype)
        m = (g[..., 0, :] * (oa @ w_attn_o) + g[..., 1, :] * (os5 @ w_s5_o)
             + g[..., 2, :] * (ossd @ w_ssd_o))
        return m @ w_out

    y = merge(o_attn, o_s5, o_ssd, gates)
    yc = merge(oc_attn, oc_s5, oc_ssd, gates_c) if with_ctx_out else None
    return y, yc


def setup_inputs(seed: int = 0) -> dict:
    key = jax.random.key(seed)
    ks = jax.random.split(key, 32)
    f32 = jnp.float32
    nrm = lambda k, shape, s: jax.random.normal(k, shape, f32) * s
    log_u = lambda k, shape, lo, hi: jax.random.uniform(k, shape, f32, np.log(lo), np.log(hi))
    dt0 = jnp.exp(log_u(ks[26], (DEPTH, 2, SSD_HEADS), 0.001, 0.1))
    return {
        'x': nrm(ks[0], (BATCH, SEQ, D_MODEL), 1.0),
        'c': nrm(ks[1], (BATCH, D_MODEL), 1.0),
        'ctx': nrm(ks[2], (BATCH, CTX_LEN, D_MODEL), 1.0),
        'c_ctx': nrm(ks[3], (D_MODEL,), 1.0),
        'w_ada': nrm(ks[4], (DEPTH, D_MODEL, N_MOD * D_MODEL), 0.5 * D_MODEL ** -0.5),
        'b_ada': nrm(ks[5], (DEPTH, N_MOD * D_MODEL), 0.01),
        'norm_g': 1.0 + nrm(ks[6], (DEPTH, 3, D_MODEL), 0.02),
        'w_ffn_gate': nrm(ks[7], (DEPTH, 2, D_MODEL, FFN_DIM), D_MODEL ** -0.5),
        'w_ffn_up': nrm(ks[8], (DEPTH, 2, D_MODEL, FFN_DIM), D_MODEL ** -0.5),
        'w_ffn_down': nrm(ks[9], (DEPTH, 2, FFN_DIM, D_MODEL), FFN_DIM ** -0.5),
        'w_in': nrm(ks[10], (DEPTH, D_MODEL, IN_DIM), D_MODEL ** -0.5),
        'qk_norm_g': 1.0 + nrm(ks[11], (DEPTH, 2, HEAD_DIM), 0.02),
        'attn_sink': nrm(ks[12], (DEPTH, ATTN_HEADS), 0.5),
        'w_attn_o': nrm(ks[13], (DEPTH, Q_DIM, D_MODEL), Q_DIM ** -0.5),
        's5_a_re': -0.5 + nrm(ks[14], (DEPTH, 2, S5_GROUPS, S5_STATE), 0.02),
        's5_a_im': jnp.pi * jnp.arange(S5_STATE, dtype=f32) + nrm(ks[15], (DEPTH, 2, S5_GROUPS, S5_STATE), 0.02),
        's5_log_step': log_u(ks[16], (DEPTH, 2, S5_GROUPS), 0.001, 0.1),
        's5_b_re': nrm(ks[17], (DEPTH, S5_GROUPS, S5_STATE, S5_GROUP), (2 * S5_GROUP) ** -0.5),
        's5_b_im': nrm(ks[18], (DEPTH, S5_GROUPS, S5_STATE, S5_GROUP), (2 * S5_GROUP) ** -0.5),
        's5_c_re': nrm(ks[19], (DEPTH, S5_GROUPS, S5_GROUP, S5_STATE), S5_STATE ** -0.5),
        's5_c_im': nrm(ks[20], (DEPTH, S5_GROUPS, S5_GROUP, S5_STATE), S5_STATE ** -0.5),
        's5_d': nrm(ks[21], (DEPTH, S5_DIM), 0.5),
        's5_w_glu': nrm(ks[22], (DEPTH, S5_DIM, S5_DIM), S5_DIM ** -0.5),
        'w_s5_o': nrm(ks[23], (DEPTH, S5_DIM, D_MODEL), S5_DIM ** -0.5),
        'ssd_conv_w': nrm(ks[24], (DEPTH, SSD_CONV, SSD_XBC), SSD_CONV ** -0.5),
        'ssd_conv_b': nrm(ks[25], (DEPTH, SSD_XBC), 0.01),
        'ssd_dt_bias': dt0 + jnp.log(-jnp.expm1(-dt0)),
        'ssd_a_log': jnp.log(jax.random.uniform(ks[27], (DEPTH, 2, SSD_HEADS), f32, 1.0, 16.0)),
        'ssd_d': 1.0 + nrm(ks[28], (DEPTH, SSD_HEADS), 0.1),
        'ssd_norm_g': 1.0 + nrm(ks[29], (DEPTH, SSD_DIM), 0.02),
        'w_ssd_o': nrm(ks[30], (DEPTH, SSD_DIM, D_MODEL), SSD_DIM ** -0.5),
        'w_out': nrm(ks[31], (DEPTH, D_MODEL, D_MODEL), D_MODEL ** -0.5),
    }


def reference(x, c, ctx, c_ctx, w_ada, b_ada, norm_g, w_ffn_gate, w_ffn_up, w_ffn_down, w_in,
              qk_norm_g, attn_sink, w_attn_o, s5_a_re, s5_a_im, s5_log_step, s5_b_re, s5_b_im,
              s5_c_re, s5_c_im, s5_d, s5_w_glu, w_s5_o, ssd_conv_w, ssd_conv_b, ssd_dt_bias,
              ssd_a_log, ssd_d, ssd_norm_g, w_ssd_o, w_out):
    bsz, n_tok = x.shape[0], x.shape[1]
    tabs = axial_rope_tables(n_tok)
    sc = jax.nn.silu(c.astype(jnp.float32))
    scc = jax.nn.silu(c_ctx.astype(jnp.float32))
    h, hc = x, ctx
    for l in range(DEPTH):
        with_ctx_out = l < DEPTH - 1
        mod = (sc @ w_ada[l] + b_ada[l]).astype(x.dtype).reshape(bsz, N_MOD, 1, D_MODEL)
        modc = (scc @ w_ada[l] + b_ada[l]).astype(x.dtype).reshape(N_MOD, D_MODEL)
        h = h + 0.5 * mod[:, 2] * swiglu(modulate(h, norm_g[l, 0], mod[:, 0], mod[:, 1]),
                                         w_ffn_gate[l, 0], w_ffn_up[l, 0], w_ffn_down[l, 0])
        hc = hc + 0.5 * modc[2] * swiglu(modulate(hc, norm_g[l, 0], modc[0], modc[1]),
                                         w_ffn_gate[l, 0], w_ffn_up[l, 0], w_ffn_down[l, 0])
        u = modulate(h, norm_g[l, 1], mod[:, 3], mod[:, 4])
        uc = modulate(hc, norm_g[l, 1], modc[3], modc[4])
        y, yc = token_mixers(u, uc, tabs, with_ctx_out, w_in[l], qk_norm_g[l], attn_sink[l], w_attn_o[l],
                             s5_a_re[l], s5_a_im[l], s5_log_step[l], s5_b_re[l], s5_b_im[l], s5_c_re[l],
                             s5_c_im[l], s5_d[l], s5_w_glu[l], w_s5_o[l], ssd_conv_w[l], ssd_conv_b[l],
                             ssd_dt_bias[l], ssd_a_log[l], ssd_d[l], ssd_norm_g[l], w_ssd_o[l], w_out[l])
        h = h + mod[:, 5] * y
        h = h + 0.5 * mod[:, 8] * swiglu(modulate(h, norm_g[l, 2], mod[:, 6], mod[:, 7]),
                                         w_ffn_gate[l, 1], w_ffn_up[l, 1], w_ffn_down[l, 1])
        if with_ctx_out:
            hc = hc + modc[5] * yc
            hc = hc + 0.5 * modc[8] * swiglu(modulate(hc, norm_g[l, 2], modc[6], modc[7]),
                                             w_ffn_gate[l, 1], w_ffn_up[l, 1], w_ffn_down[l, 1])
    return h
```

```python
import functools

import numpy as np
import jax
import jax.numpy as jnp
from jax import lax
from jax.experimental import pallas as pl
from jax.experimental.pallas import tpu as pltpu

F32 = jnp.float32
BF16 = jnp.bfloat16
HIGHEST = lax.Precision.HIGHEST

GRID_W = 64
N_MOD = 9
NORM_EPS = 1e-6
ATTN_HEADS = 12
ATTN_KV_HEADS = 4
HEAD_DIM = 128
ATTN_BLOCK = 128
ROPE_BASE = 10000.0
S5_DIM = 1024
S5_GROUP = 16
S5_GROUPS = S5_DIM // S5_GROUP
S5_STATE = 64
SSD_DIM = 1536
SSD_HEAD_DIM = 64
SSD_HEADS = SSD_DIM // SSD_HEAD_DIM
SSD_GROUPS = 4
SSD_STATE = 128
SSD_CONV = 5
N_BRANCH = 3
Q_DIM = ATTN_HEADS * HEAD_DIM
KV_DIM = ATTN_KV_HEADS * HEAD_DIM
SSD_BC = SSD_GROUPS * SSD_STATE
SSD_XBC = SSD_DIM + 2 * SSD_BC

VMEM_LIMIT_BYTES = 56 * 1024 * 1024
LANES = 128
SUBLANES = 8

COL_QKV = 0
COL_XBC = Q_DIM + 2 * KV_DIM
COL_S5 = COL_XBC + SSD_XBC
COL_Z = COL_S5 + S5_DIM
COL_GATE = COL_Z + SSD_DIM
COL_DT = COL_GATE + N_BRANCH * 2048
PROJ_TN = 2048

ROW_TILE = 512
MERGE_TILE = 256
S5_T = 64
SSD_Q = 128


def _params(*sem):
    return pltpu.CompilerParams(dimension_semantics=sem, vmem_limit_bytes=VMEM_LIMIT_BYTES)


def _silu(v):
    return v * jax.nn.sigmoid(v)


def _rms_mod(x, g, shift, scale):
    y = x * lax.rsqrt(jnp.mean(x * x, axis=-1, keepdims=True) + NORM_EPS) * g
    return y * (1.0 + scale) + shift


def _ada_kernel(c_ref, w_ref, b_ref, o_ref):
    s = _silu(c_ref[...]).astype(BF16)
    o_ref[...] = jnp.dot(s, w_ref[...].astype(BF16), preferred_element_type=F32) + b_ref[...]


def _ada(c_rows, w_ada, b_ada):
    depth, d, n = w_ada.shape
    tn = 1024
    return pl.pallas_call(
        _ada_kernel,
        grid=(depth, n // tn),
        in_specs=[pl.BlockSpec((SUBLANES, d), lambda l, j: (0, 0)),
                  pl.BlockSpec((None, d, tn), lambda l, j: (l, 0, j)),
                  pl.BlockSpec((None, 1, tn), lambda l, j: (l, 0, j))],
        out_specs=pl.BlockSpec((None, SUBLANES, tn), lambda l, j: (l, 0, j)),
        out_shape=jax.ShapeDtypeStruct((depth, SUBLANES, n), F32),
        compiler_params=_params("parallel", "parallel"),
        name="ada_mod",
    )(c_rows, w_ada, b_ada.reshape(depth, 1, n))


def _ffn_kernel(h_ref, mod_ref, g_ref, wg_ref, wu_ref, wd_ref, o_ref, xn_ref, acc_ref, *, rows):
    r_shift, r_scale, r_gate, r_norm = rows
    j = pl.program_id(1)

    @pl.when(j == 0)
    def _():
        xn = _rms_mod(h_ref[...], g_ref[r_norm:r_norm + 1, :], mod_ref[r_shift:r_shift + 1, :],
                      mod_ref[r_scale:r_scale + 1, :])
        xn_ref[...] = xn.astype(BF16)
        acc_ref[...] = jnp.zeros_like(acc_ref)

    xn = xn_ref[...]
    a = jnp.dot(xn, wg_ref[...], preferred_element_type=F32)
    b = jnp.dot(xn, wu_ref[...], preferred_element_type=F32)
    t = (_silu(a) * b).astype(BF16)
    acc_ref[...] += jnp.dot(t, wd_ref[...], preferred_element_type=F32)

    @pl.when(j == pl.num_programs(1) - 1)
    def _():
        o_ref[...] = h_ref[...] + 0.5 * mod_ref[r_gate:r_gate + 1, :] * acc_ref[...]


def _ffn(h, mod, g, wg, wu, wd, *, rows, n_rows, tiles_per_batch, n_batch):
    d = h.shape[1]
    f = wg.shape[1]
    tm, tf = ROW_TILE, 512
    midx = lambda i, j: (jnp.minimum(i // tiles_per_batch, n_batch), 0, 0)
    return pl.pallas_call(
        functools.partial(_ffn_kernel, rows=rows),
        grid=(n_rows // tm, f // tf),
        in_specs=[pl.BlockSpec((tm, d), lambda i, j: (i, 0)),
                  pl.BlockSpec((None, N_MOD, d), midx),
                  pl.BlockSpec(g.shape, lambda i, j: (0, 0)),
                  pl.BlockSpec((d, tf), lambda i, j: (0, j)),
                  pl.BlockSpec((d, tf), lambda i, j: (0, j)),
                  pl.BlockSpec((tf, d), lambda i, j: (j, 0))],
        out_specs=pl.BlockSpec((tm, d), lambda i, j: (i, 0)),
        out_shape=jax.ShapeDtypeStruct((n_rows, d), F32),
        scratch_shapes=[pltpu.VMEM((tm, d), BF16), pltpu.VMEM((tm, d), F32)],
        compiler_params=_params("parallel", "arbitrary"),
        name="ffn",
    )(h, mod, g, wg, wu, wd)


def _proj_kernel(h_ref, mod_ref, g_ref, w_ref, o_ref, u_ref):
    @pl.when(pl.program_id(1) == 0)
    def _():
        u = _rms_mod(h_ref[...], g_ref[1:2, :], mod_ref[3:4, :], mod_ref[4:5, :])
        u_ref[...] = u.astype(BF16)

    o_ref[...] = jnp.dot(u_ref[...], w_ref[...], preferred_element_type=F32)


def _proj(h, mod, g, w, *, tiles_per_batch, n_batch):
    n, d = h.shape
    ncol = w.shape[1]
    tm, tn = ROW_TILE, PROJ_TN
    midx = lambda i, j: (jnp.minimum(i // tiles_per_batch, n_batch), 0, 0)
    return pl.pallas_call(
        _proj_kernel,
        grid=(n // tm, ncol // tn),
        in_specs=[pl.BlockSpec((tm, d), lambda i, j: (i, 0)),
                  pl.BlockSpec((None, N_MOD, d), midx),
                  pl.BlockSpec(g.shape, lambda i, j: (0, 0)),
                  pl.BlockSpec((d, tn), lambda i, j: (0, j))],
        out_specs=pl.BlockSpec((tm, tn), lambda i, j: (i, j)),
        out_shape=jax.ShapeDtypeStruct((n, ncol), F32),
        scratch_shapes=[pltpu.VMEM((tm, d), BF16)],
        compiler_params=_params("parallel", "arbitrary"),
        name="in_proj",
    )(h, mod, g, w)


def _prep_kernel(p_ref, cos_ref, sin_ref, g_ref, o_ref):
    cos = cos_ref[...]
    sin = sin_ref[...]
    lane = lax.broadcasted_iota(jnp.int32, cos.shape, 1)
    first_half = (lane & (HEAD_DIM // 2 - 1)) < HEAD_DIM // 4
    scale = HEAD_DIM ** -0.5
    for hd in range(ATTN_HEADS + ATTN_KV_HEADS):
        sl = slice(hd * HEAD_DIM, (hd + 1) * HEAD_DIM)
        x = p_ref[:, sl]
        g = g_ref[0:1, :] if hd < ATTN_HEADS else g_ref[1:2, :]
        y = x * lax.rsqrt(jnp.mean(x * x, axis=-1, keepdims=True) + NORM_EPS) * g
        partner = jnp.where(first_half, pltpu.roll(y, HEAD_DIM - HEAD_DIM // 4, 1), pltpu.roll(y, HEAD_DIM // 4, 1))
        y = y * cos + partner * sin
        if hd < ATTN_HEADS:
            y = y * scale
        o_ref[:, sl] = y.astype(BF16)
    vs = slice(Q_DIM + KV_DIM, Q_DIM + 2 * KV_DIM)
    o_ref[:, vs] = p_ref[:, vs].astype(BF16)


def _prep(p, cos_t, sin_t, qk_g, *, tiles_lat, tiles_per_batch):
    n = p.shape[0]
    tm = ROW_TILE
    w = Q_DIM + 2 * KV_DIM
    tidx = lambda i: (jnp.where(i < tiles_lat, i % tiles_per_batch, tiles_per_batch), 0)
    return pl.pallas_call(
        _prep_kernel,
        grid=(n // tm,),
        in_specs=[pl.BlockSpec((tm, w), lambda i: (i, 0)),
                  pl.BlockSpec((tm, HEAD_DIM), tidx),
                  pl.BlockSpec((tm, HEAD_DIM), tidx),
                  pl.BlockSpec(qk_g.shape, lambda i: (0, 0))],
        out_specs=pl.BlockSpec((tm, w), lambda i: (i, 0)),
        out_shape=jax.ShapeDtypeStruct((n, w), BF16),
        compiler_params=_params("parallel"),
        name="qk_prep",
    )(p, cos_t, sin_t, qk_g)


def _attn_kernel(sink_ref, q_ref, kp_ref, ko_ref, kn_ref, vp_ref, vo_ref, vn_ref, kc_ref, vc_ref, o_ref, *, nb):
    n = pl.program_id(1)
    blk = ATTN_BLOCK
    grp = ATTN_HEADS // ATTN_KV_HEADS
    is_lat = n < nb
    has_prev = jnp.logical_and(is_lat, n > 0)
    has_next = jnp.logical_and(is_lat, n < nb - 1)
    rows = grp * blk
    r = lax.broadcasted_iota(jnp.int32, (rows, 3 * blk), 0)
    kk = lax.broadcasted_iota(jnp.int32, (rows, 3 * blk), 1)
    i = r & (blk - 1)
    far = 4 * blk
    off_prev = jnp.where(has_prev, 0, far)
    off_own = jnp.where(is_lat, 0, far)
    off_next = jnp.where(has_next, 0, far)
    mask = (((kk < blk) & (kk >= i + off_prev))
            | ((kk >= blk + off_own) & (kk < 2 * blk))
            | ((kk >= 2 * blk) & (kk - 2 * blk <= i - off_next)))
    rr = lax.broadcasted_iota(jnp.int32, (rows, 1), 0)
    nt = (((1,), (1,)), ((), ()))
    for kv in range(ATTN_KV_HEADS):
        ks = slice(kv * HEAD_DIM, (kv + 1) * HEAD_DIM)
        qh = jnp.concatenate([q_ref[:, (kv * grp + g) * HEAD_DIM:(kv * grp + g + 1) * HEAD_DIM] for g in range(grp)],
                             axis=0)
        kl = jnp.concatenate([kp_ref[:, ks], ko_ref[:, ks], kn_ref[:, ks]], axis=0)
        vl = jnp.concatenate([vp_ref[:, ks], vo_ref[:, ks], vn_ref[:, ks]], axis=0)
        s_loc = lax.dot_general(qh, kl, nt, preferred_element_type=F32)
        s_ctx = lax.dot_general(qh, kc_ref[:, ks], nt, preferred_element_type=F32)
        s_loc = jnp.where(mask, s_loc, -jnp.inf)
        sink = jnp.full((rows, 1), sink_ref[kv * grp], F32)
        for g in range(1, grp):
            sink = jnp.where(rr >= g * blk, sink_ref[kv * grp + g], sink)
        m = jnp.maximum(jnp.maximum(jnp.max(s_loc, axis=-1, keepdims=True),
                                    jnp.max(s_ctx, axis=-1, keepdims=True)), sink)
        p_loc = jnp.exp(s_loc - m)
        p_ctx = jnp.exp(s_ctx - m)
        denom = (jnp.sum(p_loc, axis=-1, keepdims=True) + jnp.sum(p_ctx, axis=-1, keepdims=True)
                 + jnp.exp(sink - m))
        o = (jnp.dot(p_loc.astype(BF16), vl, preferred_element_type=F32)
             + jnp.dot(p_ctx.astype(BF16), vc_ref[:, ks], preferred_element_type=F32))
        o = o * (1.0 / denom)
        for g in range(grp):
            hs = slice((kv * grp + g) * HEAD_DIM, (kv * grp + g + 1) * HEAD_DIM)
            o_ref[:, hs] = o[g * blk:(g + 1) * blk, :].astype(BF16)


def _attention(qkv, sink, *, n_batch, seq, ctx_len, with_ctx):
    n = qkv.shape[0]
    blk = ATTN_BLOCK
    nb = seq // blk
    ncb = ctx_len // blk
    steps = nb + (ncb if with_ctx else 0)
    ctx0 = n_batch * nb
    kcol = Q_DIM // KV_DIM

    def own(b, s):
        return jnp.where(s < nb, b * nb + s, ctx0 + b * ncb + (s - nb))

    def prev(b, s):
        return jnp.where(s < nb, b * nb + jnp.maximum(s - 1, 0), ctx0 + b * ncb)

    def nxt(b, s):
        return jnp.where(s < nb, b * nb + jnp.minimum(s + 1, nb - 1), ctx0 + b * ncb)

    ctx_blk = lambda b, s: ((n_batch * seq) // ctx_len + b)
    kv_spec = lambda f, col: pl.BlockSpec((blk, KV_DIM), lambda b, s: (f(b, s), col))
    return pl.pallas_call(
        functools.partial(_attn_kernel, nb=nb),
        grid=(n_batch, steps),
        in_specs=[pl.BlockSpec(memory_space=pltpu.SMEM),
                  pl.BlockSpec((blk, Q_DIM), lambda b, s: (own(b, s), 0)),
                  kv_spec(prev, kcol), kv_spec(own, kcol), kv_spec(nxt, kcol),
                  kv_spec(prev, kcol + 1), kv_spec(own, kcol + 1), kv_spec(nxt, kcol + 1),
                  pl.BlockSpec((ctx_len, KV_DIM), lambda b, s: (ctx_blk(b, s), kcol)),
                  pl.BlockSpec((ctx_len, KV_DIM), lambda b, s: (ctx_blk(b, s), kcol + 1))],
        out_specs=pl.BlockSpec((blk, Q_DIM), lambda b, s: (own(b, s), 0)),
        out_shape=jax.ShapeDtypeStruct((n, Q_DIM), BF16),
        compiler_params=_params("parallel", "parallel"),
        name="window_attn",
    )(sink, qkv, qkv, qkv, qkv, qkv, qkv, qkv, qkv, qkv)


def _s5_tables(a_re, a_im, log_step, b_re, b_im, c_re, c_im):
    t = S5_T
    step = jnp.exp(log_step.astype(F32))[..., None]
    a_re = a_re.astype(F32)
    a_im = a_im.astype(F32)
    k = jnp.arange(t + 1, dtype=F32)[:, None, None, None]
    mag = jnp.exp(k * (a_re * step)[None])
    ang = k * (a_im * step)[None]
    pr, pi = mag * jnp.cos(ang), mag * jnp.sin(ang)
    den = a_re * a_re + a_im * a_im
    nr, ni = pr[1] - 1.0, pi[1]
    wr = (nr * a_re + ni * a_im) / den
    wi = (ni * a_re - nr * a_im) / den
    b_re = b_re.astype(F32)[None]
    b_im = b_im.astype(F32)[None]
    wbr = wr[..., None] * b_re - wi[..., None] * b_im
    wbi = wr[..., None] * b_im + wi[..., None] * b_re
    c_re = c_re.astype(F32)[None, None]
    c_im = c_im.astype(F32)[None, None]
    cpr = c_re * pr[:, :, :, None, :] - c_im * pi[:, :, :, None, :]
    cpi = c_re * pi[:, :, :, None, :] + c_im * pr[:, :, :, None, :]
    kern = (jnp.einsum('kdgop,dgpi->kdgoi', cpr, wbr, precision=HIGHEST)
            - jnp.einsum('kdgop,dgpi->kdgoi', cpi, wbi, precision=HIGHEST))
    kf, kb = kern[:t, 0], kern[:t, 1]
    kfull = jnp.concatenate([kb[:0:-1], (kf[0] + kb[0])[None], kf[1:]], axis=0)
    idx = jnp.arange(t)[None, :] - jnp.arange(t)[:, None] + (t - 1)
    m = kfull[idx]
    m = jnp.transpose(m, (2, 0, 4, 1, 3)).reshape(S5_GROUPS, t * S5_GROUP, t * S5_GROUP)

    def cmul(xr, xi, yr, yi):
        return xr * yr - xi * yi, xr * yi + xi * yr

    ffr, ffi = cmul(pr[t - 1::-1, 0][..., None][:t], pi[t - 1::-1, 0][..., None][:t], wbr[0][None], wbi[0][None])
    fbr, fbi = cmul(pr[:t, 1][..., None], pi[:t, 1][..., None], wbr[1][None], wbi[1][None])
    f = jnp.concatenate([ffr, fbr, ffi, fbi], axis=2)
    f = jnp.transpose(f, (1, 0, 3, 2)).reshape(S5_GROUPS, t * S5_GROUP, 4 * S5_STATE)
    efr, efi = cpr[1:t + 1, 0], -cpi[1:t + 1, 0]
    ebr, ebi = cpr[t:0:-1, 1], -cpi[t:0:-1, 1]
    e = jnp.concatenate([efr, ebr, efi, ebi], axis=3)
    e = jnp.transpose(e, (1, 3, 0, 2)).reshape(S5_GROUPS, 4 * S5_STATE, t * S5_GROUP)
    dec_re = jnp.concatenate([pr[t, 0], pr[t, 1]], axis=-1)
    dec_im = jnp.concatenate([pi[t, 0], pi[t, 1]], axis=-1)
    return m.astype(BF16), f.astype(BF16), e.astype(BF16), dec_re, dec_im


def _s5_state_kernel(u_ref, f_ref, sre_ref, sim_ref):
    s = jnp.dot(u_ref[...], f_ref[...], preferred_element_type=F32)
    sre_ref[...] = s[:, :2 * S5_STATE]
    sim_ref[...] = s[:, 2 * S5_STATE:]


def _s5_rec_kernel(sre_ref, sim_ref, are_ref, aim_ref, hre_ref, him_ref, *, n_batch, nlat, nctx):
    are = are_ref[...]
    aim = aim_ref[...]
    fwd = lax.broadcasted_iota(jnp.int32, are.shape, 1) < S5_STATE
    half = S5_STATE

    def body(i, carry):
        out = []
        for b in range(n_batch):
            hr, hi = carry[b]
            lat0 = b * nlat
            ctx0 = n_batch * nlat + b * nctx
            in_ctx = i < nctx
            fi = jnp.where(in_ctx, ctx0 + i, lat0 + i - nctx)
            bi = jnp.where(in_ctx, ctx0 + nctx - 1 - i, lat0 + nlat - 1 - (i - nctx))
            hre_ref[fi, :, 0:half] = hr[:, 0:half]
            hre_ref[bi, :, half:] = hr[:, half:]
            him_ref[fi, :, 0:half] = hi[:, 0:half]
            him_ref[bi, :, half:] = hi[:, half:]
            sr = jnp.where(fwd, sre_ref[fi], sre_ref[bi])
            si = jnp.where(fwd, sim_ref[fi], sim_ref[bi])
            out.append((are * hr - aim * hi + sr, are * hi + aim * hr + si))
        return tuple(out)

    zero = jnp.zeros(are.shape, F32)
    lax.fori_loop(0, nlat + nctx, body, tuple((zero, zero) for _ in range(n_batch)))


def _s5_out_kernel(u_ref, hre_ref, him_ref, m_ref, e_ref, y_ref):
    y = jnp.dot(u_ref[...], m_ref[...], preferred_element_type=F32)
    hin = jnp.concatenate([hre_ref[...], him_ref[...]], axis=1).astype(BF16)
    y_ref[...] = y + jnp.dot(hin, e_ref[...], preferred_element_type=F32)


def _s5(u_g, tables, *, n_batch, seq, ctx_len):
    m, f, e, dec_re, dec_im = tables
    g, nc, tw = u_g.shape
    sw = 2 * S5_STATE
    gspec = lambda shape: pl.BlockSpec((None,) + shape, lambda i: (i, 0, 0))
    s_re, s_im = pl.pallas_call(
        _s5_state_kernel,
        grid=(g,),
        in_specs=[gspec((nc, tw)), gspec((tw, 2 * sw))],
        out_specs=[pl.BlockSpec((nc, sw), lambda i: (0, i)), pl.BlockSpec((nc, sw), lambda i: (0, i))],
        out_shape=[jax.ShapeDtypeStruct((nc, g * sw), F32)] * 2,
        compiler_params=_params("parallel"),
        name="s5_chunk_state",
    )(u_g, f)
    gb = SUBLANES
    blk3 = pl.BlockSpec((nc, gb, sw), lambda i: (0, i, 0))
    dspec = pl.BlockSpec((gb, sw), lambda i: (i, 0))
    h_re, h_im = pl.pallas_call(
        functools.partial(_s5_rec_kernel, n_batch=n_batch, nlat=seq // S5_T, nctx=ctx_len // S5_T),
        grid=(g // gb,),
        in_specs=[blk3, blk3, dspec, dspec],
        out_specs=[blk3, blk3],
        out_shape=[jax.ShapeDtypeStruct((nc, g, sw), F32)] * 2,
        compiler_params=_params("parallel"),
        name="s5_chunk_scan",
    )(s_re.reshape(nc, g, sw), s_im.reshape(nc, g, sw), dec_re, dec_im)
    hspec = pl.BlockSpec((nc, sw), lambda i: (0, i))
    return pl.pallas_call(
        _s5_out_kernel,
        grid=(g,),
        in_specs=[gspec((nc, tw)), hspec, hspec, gspec((tw, tw)), gspec((2 * sw, tw))],
        out_specs=gspec((nc, tw)),
        out_shape=jax.ShapeDtypeStruct((g, nc, tw), F32),
        compiler_params=_params("parallel"),
        name="s5_out",
    )(u_g, h_re.reshape(nc, g * sw), h_im.reshape(nc, g * sw), m, e)


def _ssd_kernel(*refs, rev, finish, n_lat_chunks, n_ctx_chunks):
    if finish:
        (xo_ref, xp_ref, xn_ref, dt_ref, z_ref, yb_ref, cw_ref, cb_ref, bias_ref, a_ref, ex_ref, d_ref, ng_ref,
         o_ref, xe_ref, h_ref, yd_ref) = refs
    else:
        (xo_ref, xp_ref, xn_ref, dt_ref, cw_ref, cb_ref, bias_ref, a_ref, ex_ref,
         o_ref, xe_ref, h_ref, yd_ref) = refs
    q = SSD_Q
    halo = SUBLANES
    i = pl.program_id(1)

    @pl.when(i == 0)
    def _():
        h_ref[...] = jnp.zeros_like(h_ref)

    in_ctx = i < n_ctx_chunks
    pos = jnp.where(in_ctx, i, i - n_ctx_chunks)
    length = jnp.where(in_ctx, n_ctx_chunks, n_lat_chunks)
    if rev:
        pos = length - 1 - pos
    has_prev = pos > 0
    has_next = pos < length - 1

    xe_ref[0:halo, :] = xp_ref[...] * has_prev.astype(F32)
    xe_ref[halo:halo + q, :] = xo_ref[...]
    xe_ref[halo + q:, :] = xn_ref[...] * has_next.astype(F32)
    acc = cb_ref[...] + cw_ref[0:1, :] * xe_ref[halo - 2:halo - 2 + q, :]
    for k in range(1, SSD_CONV):
        acc = acc + cw_ref[k:k + 1, :] * xe_ref[halo - 2 + k:halo - 2 + k + q, :]
    xbc = _silu(acc)
    xs = xbc[:, :SSD_DIM]
    bm_f = xbc[:, SSD_DIM:SSD_DIM + SSD_BC]
    bm = bm_f.astype(BF16)
    cm = xbc[:, SSD_DIM + SSD_BC:].astype(BF16)

    raw = dt_ref[...] + bias_ref[...]
    dt = jnp.maximum(raw, 0.0) + jnp.log1p(jnp.exp(-jnp.abs(raw)))
    ad = dt * a_ref[...]
    li = lax.broadcasted_iota(jnp.int32, (q, q), 0)
    si = lax.broadcasted_iota(jnp.int32, (q, q), 1)
    keep = (si >= li) if rev else (si <= li)
    tri = keep.astype(F32)
    a_cs = jnp.dot(tri, ad, precision=HIGHEST, preferred_element_type=F32)
    a_cs_t = a_cs.T
    edge = 0 if rev else q - 1
    tot = a_cs[edge:edge + 1, :]
    ex = ex_ref[...]
    dt_x = jnp.dot(dt, ex, precision=HIGHEST, preferred_element_type=F32)
    ds_x = jnp.dot(jnp.exp(tot - a_cs), ex, precision=HIGHEST, preferred_element_type=F32)
    do_x = jnp.dot(jnp.exp(a_cs), ex, precision=HIGHEST, preferred_element_type=F32)
    tot_x = do_x[edge:edge + 1, :]

    xd = xs * dt_x
    xd_b = xd.astype(BF16)
    xds_b = (xd * ds_x).astype(BF16)
    rep = SSD_HEADS // SSD_GROUPS
    gw = rep * SSD_HEAD_DIM
    lane0 = SSD_HEADS if rev else 0
    nt = (((1,), (1,)), ((), ()))
    y_parts = []
    for g in range(SSD_GROUPS):
        bg = bm[:, g * SSD_STATE:(g + 1) * SSD_STATE]
        cg = cm[:, g * SSD_STATE:(g + 1) * SSD_STATE]
        cb = lax.dot_general(cg, bg, nt, preferred_element_type=F32)
        for r in range(rep):
            hd = g * rep + r
            col = a_cs[:, lane0 + hd:lane0 + hd + 1]
            row = a_cs_t[lane0 + hd:lane0 + hd + 1, :]
            dec = jnp.exp(jnp.where(keep, col - row, -jnp.inf))
            yd_ref[:, hd * SSD_HEAD_DIM:(hd + 1) * SSD_HEAD_DIM] = jnp.dot(
                (cb * dec).astype(BF16), xd_b[:, hd * SSD_HEAD_DIM:(hd + 1) * SSD_HEAD_DIM],
                preferred_element_type=F32)
        gs = slice(g * gw, (g + 1) * gw)
        h_g = h_ref[:, gs]
        y_off = jnp.dot(cg, h_g.astype(BF16), preferred_element_type=F32) * do_x[:, gs]
        bg_t = bm_f[:, g * SSD_STATE:(g + 1) * SSD_STATE].T.astype(BF16)
        st = jnp.dot(bg_t, xds_b[:, gs], preferred_element_type=F32)
        h_ref[:, gs] = h_g * tot_x[:, gs] + st
        y_parts.append(y_off)
    y = yd_ref[...] + jnp.concatenate(y_parts, axis=1)

    if finish:
        y = y + yb_ref[...] + d_ref[...] * xs
        y = y * _silu(z_ref[...])
        nw = SSD_DIM // SSD_GROUPS
        outs = []
        for g in range(SSD_GROUPS):
            yg = y[:, g * nw:(g + 1) * nw]
            outs.append(yg * lax.rsqrt(jnp.mean(yg * yg, axis=-1, keepdims=True) + NORM_EPS))
        o_ref[...] = (jnp.concatenate(outs, axis=1) * ng_ref[...]).astype(BF16)
    else:
        o_ref[...] = y


def _ssd_pass(p, y_other, vecs, *, rev, n_batch, seq, ctx_len, col_dt):
    n = p.shape[0]
    q = SSD_Q
    halo = SUBLANES
    nl, ncx = seq // q, ctx_len // q
    hpq = q // halo
    finish = y_other is not None
    conv_w, conv_b, bias, a_vec, expand, d_vec, norm_g = vecs

    def chunk(b, i):
        pos_c = (ncx - 1 - i) if rev else i
        pos_l = (nl - 1 - (i - ncx)) if rev else (i - ncx)
        return jnp.where(i < ncx, n_batch * nl + b * ncx + pos_c, b * nl + pos_l)

    last_halo = n // halo - 1
    own = lambda col_blk, w: pl.BlockSpec((q, w), lambda b, i: (chunk(b, i), col_blk))
    prev = pl.BlockSpec((halo, SSD_XBC), lambda b, i: (jnp.maximum(chunk(b, i) * hpq - 1, 0), COL_XBC // SSD_XBC))
    nxt = pl.BlockSpec((halo, SSD_XBC),
                       lambda b, i: (jnp.minimum((chunk(b, i) + 1) * hpq, last_halo), COL_XBC // SSD_XBC))
    const = lambda a: pl.BlockSpec(a.shape, lambda b, i: (0, 0))
    in_specs = [own(COL_XBC // SSD_XBC, SSD_XBC), prev, nxt, own(col_dt // LANES, LANES)]
    args = [p, p, p, p]
    if finish:
        in_specs += [own(COL_Z // SSD_DIM, SSD_DIM), own(0, SSD_DIM)]
        args += [p, y_other]
    tail = [conv_w, conv_b, bias, a_vec, expand] + ([d_vec, norm_g] if finish else [])
    in_specs += [const(a) for a in tail]
    args += tail
    return pl.pallas_call(
        functools.partial(_ssd_kernel, rev=rev, finish=finish, n_lat_chunks=nl, n_ctx_chunks=ncx),
        grid=(n_batch, nl + ncx),
        in_specs=in_specs,
        out_specs=own(0, SSD_DIM),
        out_shape=jax.ShapeDtypeStruct((n, SSD_DIM), BF16 if finish else F32),
        scratch_shapes=[pltpu.VMEM((q + 2 * halo, SSD_XBC), F32),
                        pltpu.VMEM((SSD_STATE, SSD_DIM), F32),
                        pltpu.VMEM((q, SSD_DIM), F32)],
        compiler_params=_params("parallel", "arbitrary"),
        name="ssd_bwd" if rev else "ssd_fwd",
    )(*args)


def _ssd(p, conv_w, conv_b, dt_bias, a_log, d_skip, norm_g, *, n_batch, seq, ctx_len, col_dt):
    cw = jnp.zeros((SUBLANES, SSD_XBC), F32).at[:SSD_CONV].set(conv_w.astype(F32))
    cb = conv_b.astype(F32).reshape(1, SSD_XBC)
    bias = jnp.zeros((1, LANES), F32).at[0, :2 * SSD_HEADS].set(dt_bias.astype(F32).reshape(-1))
    a = -jnp.exp(a_log.astype(F32))
    d_vec = jnp.repeat(d_skip.astype(F32), SSD_HEAD_DIM).reshape(1, SSD_DIM)
    ng = norm_g.astype(F32).reshape(1, SSD_DIM)
    head_of_channel = jnp.arange(SSD_DIM) // SSD_HEAD_DIM
    outs = {}
    for rev in (True, False):
        lane0 = SSD_HEADS if rev else 0
        a_vec = jnp.zeros((1, LANES), F32).at[0, lane0:lane0 + SSD_HEADS].set(a[1 if rev else 0])
        expand = (jnp.arange(LANES)[:, None] == (head_of_channel[None, :] + lane0)).astype(F32)
        outs[rev] = _ssd_pass(p, None if rev else outs[True], (cw, cb, bias, a_vec, expand, d_vec, ng),
                              rev=rev, n_batch=n_batch, seq=seq, ctx_len=ctx_len, col_dt=col_dt)
    return outs[False]


def _merge_kernel(h_ref, mod_ref, ys5_ref, u_ref, d_ref, wglu_ref, oa_ref, od_ref, g0_ref, g1_ref, g2_ref,
                  wa_ref, ws_ref, wd_ref, wo_ref, o_ref, os5_ref, acc_ref):
    j = pl.program_id(1)

    @pl.when(j == 0)
    def _():
        y = ys5_ref[...] + d_ref[...] * u_ref[...]
        g = 0.5 * y * (1.0 + jnp.tanh(np.sqrt(2.0 / np.pi).astype(np.float32) * (y + 0.044715 * (y * y * y))))
        gate = jax.nn.sigmoid(jnp.dot(g.astype(BF16), wglu_ref[...], preferred_element_type=F32))
        os5_ref[...] = (g * gate).astype(BF16)
        acc_ref[...] = jnp.zeros_like(acc_ref)

    m = (jax.nn.sigmoid(g0_ref[...]) * jnp.dot(oa_ref[...], wa_ref[...], preferred_element_type=F32)
         + jax.nn.sigmoid(g1_ref[...]) * jnp.dot(os5_ref[...], ws_ref[...], preferred_element_type=F32)
         + jax.nn.sigmoid(g2_ref[...]) * jnp.dot(od_ref[...], wd_ref[...], preferred_element_type=F32))
    acc_ref[...] += jnp.dot(m.astype(BF16), wo_ref[...], preferred_element_type=F32)

    @pl.when(j == pl.num_programs(1) - 1)
    def _():
        o_ref[...] = h_ref[...] + mod_ref[5:6, :] * acc_ref[...]


def _merge(h, mod, p, ys5, oa, ossd, d_s5, w_glu, w_a, w_s, w_d, w_o, *, n_rows, tiles_per_batch, n_batch):
    d = h.shape[1]
    tm, tc = MERGE_TILE, 512
    nj = d // tc
    midx = lambda i, j: (jnp.minimum(i // tiles_per_batch, n_batch), 0, 0)
    row = lambda w: pl.BlockSpec((tm, w), lambda i, j: (i, 0))
    gate = lambda br: pl.BlockSpec((tm, tc), lambda i, j: (i, (COL_GATE + br * d) // tc + j))
    wcol = lambda k: pl.BlockSpec((k, tc), lambda i, j: (0, j))
    return pl.pallas_call(
        _merge_kernel,
        grid=(n_rows // tm, nj),
        in_specs=[row(d), pl.BlockSpec((None, N_MOD, d), midx), row(S5_DIM),
                  pl.BlockSpec((tm, S5_DIM), lambda i, j: (i, COL_S5 // S5_DIM)),
                  pl.BlockSpec((1, S5_DIM), lambda i, j: (0, 0)),
                  pl.BlockSpec((S5_DIM, S5_DIM), lambda i, j: (0, 0)),
                  row(Q_DIM), row(SSD_DIM), gate(0), gate(1), gate(2),
                  wcol(Q_DIM), wcol(S5_DIM), wcol(SSD_DIM),
                  pl.BlockSpec((tc, d), lambda i, j: (j, 0))],
        out_specs=row(d),
        out_shape=jax.ShapeDtypeStruct((n_rows, d), F32),
        scratch_shapes=[pltpu.VMEM((tm, S5_DIM), BF16), pltpu.VMEM((tm, d), F32)],
        compiler_params=_params("parallel", "arbitrary"),
        name="merge",
    )(h, mod, ys5, p, d_s5, w_glu, oa, ossd, p, p, p, w_a, w_s, w_d, w_o)


def _rope_tables(seq, extra_rows):
    rows = seq // GRID_W
    row = jnp.repeat(jnp.arange(rows, dtype=F32), GRID_W)
    col = jnp.tile(jnp.arange(GRID_W, dtype=F32), rows)
    axis_dim = HEAD_DIM // 2
    inv_freq = ROPE_BASE ** (-jnp.arange(0, axis_dim, 2, dtype=F32) / axis_dim)
    ang_r = row[:, None] * inv_freq[None, :]
    ang_c = col[:, None] * inv_freq[None, :]
    cos = jnp.concatenate([jnp.cos(ang_r)] * 2 + [jnp.cos(ang_c)] * 2, axis=1)
    sin = jnp.concatenate([-jnp.sin(ang_r), jnp.sin(ang_r), -jnp.sin(ang_c), jnp.sin(ang_c)], axis=1)
    cos = jnp.concatenate([cos, jnp.ones((extra_rows, HEAD_DIM), F32)], axis=0)
    sin = jnp.concatenate([sin, jnp.zeros((extra_rows, HEAD_DIM), F32)], axis=0)
    return cos, sin


def _permute_w_in(w, d):
    q0, k0, v0 = 0, Q_DIM, Q_DIM + KV_DIM
    s0 = Q_DIM + 2 * KV_DIM
    z0 = s0 + S5_DIM
    x0 = z0 + SSD_DIM
    t0 = x0 + SSD_XBC
    g0 = t0 + 2 * SSD_HEADS
    g1 = g0 + N_BRANCH * d
    used = COL_GATE + N_BRANCH * d + 2 * SSD_HEADS
    total = -(-used // PROJ_TN) * PROJ_TN
    parts = [w[:, q0:s0], w[:, x0:t0], w[:, s0:z0], w[:, z0:x0], w[:, g0:g1], w[:, t0:g0],
             jnp.zeros((w.shape[0], total - used), w.dtype)]
    return jnp.concatenate(parts, axis=1).astype(BF16)


def kernel(x, c, ctx, c_ctx, w_ada, b_ada, norm_g, w_ffn_gate, w_ffn_up, w_ffn_down, w_in, qk_norm_g, attn_sink,
           w_attn_o, s5_a_re, s5_a_im, s5_log_step, s5_b_re, s5_b_im, s5_c_re, s5_c_im, s5_d, s5_w_glu, w_s5_o,
           ssd_conv_w, ssd_conv_b, ssd_dt_bias, ssd_a_log, ssd_d, ssd_norm_g, w_ssd_o, w_out):
    n_batch, seq, d = x.shape
    ctx_len = ctx.shape[1]
    depth = w_ada.shape[0]
    n_lat, n_ctx = n_batch * seq, n_batch * ctx_len
    n = n_lat + n_ctx
    assert seq % ROW_TILE == 0 and n_ctx % ROW_TILE == 0 and ctx_len % SSD_Q == 0 and d % PROJ_TN == 0
    assert n_batch + 1 <= SUBLANES
    col_dt = COL_GATE + N_BRANCH * d

    h = jnp.concatenate([x.reshape(n_lat, d), ctx.reshape(n_ctx, d)], axis=0)
    c_rows = jnp.zeros((SUBLANES, d), F32).at[:n_batch].set(c.astype(F32)).at[n_batch].set(c_ctx.astype(F32))
    mods = _ada(c_rows, w_ada, b_ada)[:, :n_batch + 1].reshape(depth, n_batch + 1, N_MOD, d)
    cos_t, sin_t = _rope_tables(seq, ROW_TILE)

    tpb = seq // ROW_TILE
    for l in range(depth):
        last = l == depth - 1
        ffn_w = lambda k: (w_ffn_gate[l, k].astype(BF16), w_ffn_up[l, k].astype(BF16), w_ffn_down[l, k].astype(BF16))
        g = norm_g[l].astype(F32)
        h = _ffn(h, mods[l], g, *ffn_w(0), rows=(0, 1, 2, 0), n_rows=n, tiles_per_batch=tpb, n_batch=n_batch)
        p = _proj(h, mods[l], g, _permute_w_in(w_in[l], d), tiles_per_batch=tpb, n_batch=n_batch)

        qkv = _prep(p, cos_t, sin_t, qk_norm_g[l].astype(F32), tiles_lat=n_lat // ROW_TILE, tiles_per_batch=tpb)
        oa = _attention(qkv, attn_sink[l].astype(F32), n_batch=n_batch, seq=seq, ctx_len=ctx_len, with_ctx=not last)

        nc = n // S5_T
        u_g = p[:, COL_S5:COL_S5 + S5_DIM].reshape(nc, S5_T, S5_GROUPS, S5_GROUP)
        u_g = jnp.transpose(u_g, (2, 0, 1, 3)).reshape(S5_GROUPS, nc, S5_T * S5_GROUP).astype(BF16)
        tables = _s5_tables(s5_a_re[l], s5_a_im[l], s5_log_step[l], s5_b_re[l], s5_b_im[l], s5_c_re[l], s5_c_im[l])
        y_g = _s5(u_g, tables, n_batch=n_batch, seq=seq, ctx_len=ctx_len)
        ys5 = jnp.transpose(y_g.reshape(S5_GROUPS, nc, S5_T, S5_GROUP), (1, 2, 0, 3)).reshape(n, S5_DIM)

        ossd = _ssd(p, ssd_conv_w[l], ssd_conv_b[l], ssd_dt_bias[l], ssd_a_log[l], ssd_d[l], ssd_norm_g[l],
                    n_batch=n_batch, seq=seq, ctx_len=ctx_len, col_dt=col_dt)

        n_out = n_lat if last else n
        h = _merge(h, mods[l], p, ys5, oa, ossd, s5_d[l].astype(F32).reshape(1, S5_DIM), s5_w_glu[l].astype(BF16),
                   w_attn_o[l].astype(BF16), w_s5_o[l].astype(BF16), w_ssd_o[l].astype(BF16), w_out[l].astype(BF16),
                   n_rows=n_out, tiles_per_batch=seq // MERGE_TILE, n_batch=n_batch)
        h = _ffn(h, mods[l], g, *ffn_w(1), rows=(6, 7, 8, 2), n_rows=n_out, tiles_per_batch=tpb, n_batch=n_batch)
    return h.reshape(n_batch, seq, d)
```

```python
import functools

import numpy as np
import jax
import jax.numpy as jnp
from jax import lax
from jax.experimental import pallas as pl
from jax.experimental.pallas import tpu as pltpu

F32 = jnp.float32
BF16 = jnp.bfloat16
HIGHEST = lax.Precision.HIGHEST

GRID_W = 64
N_MOD = 9
NORM_EPS = 1e-6
ATTN_HEADS = 12
ATTN_KV_HEADS = 4
HEAD_DIM = 128
ATTN_BLOCK = 128
ROPE_BASE = 10000.0
S5_DIM = 1024
S5_GROUP = 16
S5_GROUPS = S5_DIM // S5_GROUP
S5_STATE = 64
SSD_DIM = 1536
SSD_HEAD_DIM = 64
SSD_HEADS = SSD_DIM // SSD_HEAD_DIM
SSD_GROUPS = 4
SSD_STATE = 128
SSD_CONV = 5
N_BRANCH = 3
Q_DIM = ATTN_HEADS * HEAD_DIM
KV_DIM = ATTN_KV_HEADS * HEAD_DIM
SSD_BC = SSD_GROUPS * SSD_STATE
SSD_XBC = SSD_DIM + 2 * SSD_BC

VMEM_LIMIT_BYTES = 56 * 1024 * 1024
LANES = 128
SUBLANES = 8

COL_QKV = 0
COL_XBC = Q_DIM + 2 * KV_DIM
COL_S5 = COL_XBC + SSD_XBC
COL_Z = COL_S5 + S5_DIM
COL_GATE = COL_Z + SSD_DIM
COL_DT = COL_GATE + N_BRANCH * 2048
PROJ_TN = 2048

ROW_TILE = 512
S5_T = 32
SSD_Q = 128


def _params(*sem):
    return pltpu.CompilerParams(dimension_semantics=sem, vmem_limit_bytes=VMEM_LIMIT_BYTES)


def _sigmoid(v):
    return 0.5 * (1.0 + jnp.tanh(0.5 * v))


def _silu(v):
    return v * _sigmoid(v)


def _split_bf16(v):
    hi = v.astype(BF16)
    return hi, (v - hi.astype(F32)).astype(BF16)


def _rms_mod(x, g, shift, scale):
    y = x * lax.rsqrt(jnp.mean(x * x, axis=-1, keepdims=True) + NORM_EPS) * g
    return y * (1.0 + scale) + shift


def _ada_kernel(c_ref, w_ref, b_ref, o_ref):
    s = _silu(c_ref[...]).astype(BF16)
    o_ref[...] = jnp.dot(s, w_ref[...].astype(BF16), preferred_element_type=F32) + b_ref[...]


def _ada(c_rows, w_ada, b_ada):
    depth, d, n = w_ada.shape
    tn = 1024
    return pl.pallas_call(
        _ada_kernel,
        grid=(depth, n // tn),
        in_specs=[pl.BlockSpec((SUBLANES, d), lambda l, j: (0, 0)),
                  pl.BlockSpec((None, d, tn), lambda l, j: (l, 0, j)),
                  pl.BlockSpec((None, 1, tn), lambda l, j: (l, 0, j))],
        out_specs=pl.BlockSpec((None, SUBLANES, tn), lambda l, j: (l, 0, j)),
        out_shape=jax.ShapeDtypeStruct((depth, SUBLANES, n), F32),
        compiler_params=_params("parallel", "parallel"),
        name="ada_mod",
    )(c_rows, w_ada, b_ada.reshape(depth, 1, n))


def _ffn_kernel(h_ref, mod_ref, g_ref, wg_ref, wu_ref, wd_ref, o_ref, xn_ref, acc_ref, *, rows):
    r_shift, r_scale, r_gate, r_norm = rows
    j = pl.program_id(1)

    @pl.when(j == 0)
    def _():
        xn = _rms_mod(h_ref[...], g_ref[r_norm:r_norm + 1, :], mod_ref[r_shift:r_shift + 1, :],
                      mod_ref[r_scale:r_scale + 1, :])
        xn_ref[...] = xn.astype(BF16)
        acc_ref[...] = jnp.zeros_like(acc_ref)

    xn = xn_ref[...]
    a = jnp.dot(xn, wg_ref[...], preferred_element_type=F32)
    b = jnp.dot(xn, wu_ref[...], preferred_element_type=F32)
    t = (_silu(a) * b).astype(BF16)
    acc_ref[...] += jnp.dot(t, wd_ref[...], preferred_element_type=F32)

    @pl.when(j == pl.num_programs(1) - 1)
    def _():
        o_ref[...] = h_ref[...] + 0.5 * mod_ref[r_gate:r_gate + 1, :] * acc_ref[...]


def _ffn(h, mod, g, wg, wu, wd, *, rows, n_rows, tiles_per_batch, n_batch):
    d = h.shape[1]
    f = wg.shape[1]
    tm, tf = ROW_TILE, 512
    midx = lambda i, j: (jnp.minimum(i // tiles_per_batch, n_batch), 0, 0)
    return pl.pallas_call(
        functools.partial(_ffn_kernel, rows=rows),
        grid=(n_rows // tm, f // tf),
        in_specs=[pl.BlockSpec((tm, d), lambda i, j: (i, 0)),
                  pl.BlockSpec((None, N_MOD, d), midx),
                  pl.BlockSpec(g.shape, lambda i, j: (0, 0)),
                  pl.BlockSpec((d, tf), lambda i, j: (0, j)),
                  pl.BlockSpec((d, tf), lambda i, j: (0, j)),
                  pl.BlockSpec((tf, d), lambda i, j: (j, 0))],
        out_specs=pl.BlockSpec((tm, d), lambda i, j: (i, 0)),
        out_shape=jax.ShapeDtypeStruct((n_rows, d), F32),
        scratch_shapes=[pltpu.VMEM((tm, d), BF16), pltpu.VMEM((tm, d), F32)],
        compiler_params=_params("parallel", "arbitrary"),
        name="ffn",
    )(h, mod, g, wg, wu, wd)


def _proj_kernel(h_ref, mod_ref, g_ref, w_ref, o_ref, u_ref):
    @pl.when(pl.program_id(1) == 0)
    def _():
        u = _rms_mod(h_ref[...], g_ref[1:2, :], mod_ref[3:4, :], mod_ref[4:5, :])
        u_ref[...] = u.astype(BF16)

    o_ref[...] = jnp.dot(u_ref[...], w_ref[...], preferred_element_type=F32)


def _proj(h, mod, g, w, *, tiles_per_batch, n_batch):
    n, d = h.shape
    ncol = w.shape[1]
    tm, tn = ROW_TILE, PROJ_TN
    midx = lambda i, j: (jnp.minimum(i // tiles_per_batch, n_batch), 0, 0)
    return pl.pallas_call(
        _proj_kernel,
        grid=(n // tm, ncol // tn),
        in_specs=[pl.BlockSpec((tm, d), lambda i, j: (i, 0)),
                  pl.BlockSpec((None, N_MOD, d), midx),
                  pl.BlockSpec(g.shape, lambda i, j: (0, 0)),
                  pl.BlockSpec((d, tn), lambda i, j: (0, j))],
        out_specs=pl.BlockSpec((tm, tn), lambda i, j: (i, j)),
        out_shape=jax.ShapeDtypeStruct((n, ncol), F32),
        scratch_shapes=[pltpu.VMEM((tm, d), BF16)],
        compiler_params=_params("parallel", "arbitrary"),
        name="in_proj",
    )(h, mod, g, w)


def _prep_kernel(p_ref, cos_ref, sin_ref, g_ref, o_ref):
    cos = cos_ref[...]
    sin = sin_ref[...]
    scale = HEAD_DIM ** -0.5
    mean_mat = jnp.full((HEAD_DIM, HEAD_DIM), 1.0 / HEAD_DIM, BF16)
    for hd in range(ATTN_HEADS + ATTN_KV_HEADS):
        sl = slice(hd * HEAD_DIM, (hd + 1) * HEAD_DIM)
        x = p_ref[:, sl]
        g = g_ref[0:1, :] if hd < ATTN_HEADS else g_ref[1:2, :]
        sq_hi, sq_lo = _split_bf16(x * x)
        ms = (jnp.dot(sq_hi, mean_mat, preferred_element_type=F32)
              + jnp.dot(sq_lo, mean_mat, preferred_element_type=F32))
        y = x * lax.rsqrt(ms + NORM_EPS) * g
        y = y * cos + pltpu.roll(y, HEAD_DIM // 2, 1) * sin
        if hd < ATTN_HEADS:
            y = y * scale
        o_ref[:, sl] = y.astype(BF16)
    vs = slice(Q_DIM + KV_DIM, Q_DIM + 2 * KV_DIM)
    o_ref[:, vs] = p_ref[:, vs].astype(BF16)


def _prep(p, cos_t, sin_t, qk_g, *, tiles_lat, tiles_per_batch):
    n = p.shape[0]
    tm = ROW_TILE
    w = Q_DIM + 2 * KV_DIM
    tidx = lambda i: (jnp.where(i < tiles_lat, i % tiles_per_batch, tiles_per_batch), 0)
    return pl.pallas_call(
        _prep_kernel,
        grid=(n // tm,),
        in_specs=[pl.BlockSpec((tm, w), lambda i: (i, 0)),
                  pl.BlockSpec((tm, HEAD_DIM), tidx),
                  pl.BlockSpec((tm, HEAD_DIM), tidx),
                  pl.BlockSpec(qk_g.shape, lambda i: (0, 0))],
        out_specs=pl.BlockSpec((tm, w), lambda i: (i, 0)),
        out_shape=jax.ShapeDtypeStruct((n, w), BF16),
        compiler_params=_params("parallel"),
        name="qk_prep",
    )(p, cos_t, sin_t, qk_g)


def _attn_kernel(sink_ref, q_ref, kp_ref, ko_ref, kn_ref, vp_ref, vo_ref, vn_ref, kc_ref, vc_ref, o_ref, *, nb):
    n = pl.program_id(1)
    blk = ATTN_BLOCK
    grp = ATTN_HEADS // ATTN_KV_HEADS
    is_lat = n < nb
    has_prev = jnp.logical_and(is_lat, n > 0)
    has_next = jnp.logical_and(is_lat, n < nb - 1)
    rows = grp * blk
    r = lax.broadcasted_iota(jnp.int32, (rows, 3 * blk), 0)
    kk = lax.broadcasted_iota(jnp.int32, (rows, 3 * blk), 1)
    i = r & (blk - 1)
    far = 4 * blk
    off_prev = jnp.where(has_prev, 0, far)
    off_own = jnp.where(is_lat, 0, far)
    off_next = jnp.where(has_next, 0, far)
    mask = (((kk < blk) & (kk >= i + off_prev))
            | ((kk >= blk + off_own) & (kk < 2 * blk))
            | ((kk >= 2 * blk) & (kk - 2 * blk <= i - off_next)))
    rr = lax.broadcasted_iota(jnp.int32, (rows, 1), 0)
    nt = (((1,), (1,)), ((), ()))
    for kv in range(ATTN_KV_HEADS):
        ks = slice(kv * HEAD_DIM, (kv + 1) * HEAD_DIM)
        qh = jnp.concatenate([q_ref[:, (kv * grp + g) * HEAD_DIM:(kv * grp + g + 1) * HEAD_DIM] for g in range(grp)],
                             axis=0)
        kl = jnp.concatenate([kp_ref[:, ks], ko_ref[:, ks], kn_ref[:, ks]], axis=0)
        vl = jnp.concatenate([vp_ref[:, ks], vo_ref[:, ks], vn_ref[:, ks]], axis=0)
        s_loc = lax.dot_general(qh, kl, nt, preferred_element_type=F32)
        s_ctx = lax.dot_general(qh, kc_ref[:, ks], nt, preferred_element_type=F32)
        s_loc = jnp.where(mask, s_loc, -jnp.inf)
        sink = jnp.full((rows, 1), sink_ref[kv * grp], F32)
        for g in range(1, grp):
            sink = jnp.where(rr >= g * blk, sink_ref[kv * grp + g], sink)
        m = jnp.maximum(jnp.maximum(jnp.max(s_loc, axis=-1, keepdims=True),
                                    jnp.max(s_ctx, axis=-1, keepdims=True)), sink)
        p_loc = jnp.exp(s_loc - m)
        p_ctx = jnp.exp(s_ctx - m)
        denom = (jnp.sum(p_loc, axis=-1, keepdims=True) + jnp.sum(p_ctx, axis=-1, keepdims=True)
                 + jnp.exp(sink - m))
        o = (jnp.dot(p_loc.astype(BF16), vl, preferred_element_type=F32)
             + jnp.dot(p_ctx.astype(BF16), vc_ref[:, ks], preferred_element_type=F32))
        o = o * (1.0 / denom)
        for g in range(grp):
            hs = slice((kv * grp + g) * HEAD_DIM, (kv * grp + g + 1) * HEAD_DIM)
            o_ref[:, hs] = o[g * blk:(g + 1) * blk, :].astype(BF16)


def _attention(qkv, sink, *, n_batch, seq, ctx_len, with_ctx):
    n = qkv.shape[0]
    blk = ATTN_BLOCK
    nb = seq // blk
    ncb = ctx_len // blk
    steps = nb + (ncb if with_ctx else 0)
    ctx0 = n_batch * nb
    kcol = Q_DIM // KV_DIM

    def own(b, s):
        return jnp.where(s < nb, b * nb + s, ctx0 + b * ncb + (s - nb))

    def prev(b, s):
        return jnp.where(s < nb, b * nb + jnp.maximum(s - 1, 0), ctx0 + b * ncb)

    def nxt(b, s):
        return jnp.where(s < nb, b * nb + jnp.minimum(s + 1, nb - 1), ctx0 + b * ncb)

    ctx_blk = lambda b, s: ((n_batch * seq) // ctx_len + b)
    kv_spec = lambda f, col: pl.BlockSpec((blk, KV_DIM), lambda b, s: (f(b, s), col))
    return pl.pallas_call(
        functools.partial(_attn_kernel, nb=nb),
        grid=(n_batch, steps),
        in_specs=[pl.BlockSpec(memory_space=pltpu.SMEM),
                  pl.BlockSpec((blk, Q_DIM), lambda b, s: (own(b, s), 0)),
                  kv_spec(prev, kcol), kv_spec(own, kcol), kv_spec(nxt, kcol),
                  kv_spec(prev, kcol + 1), kv_spec(own, kcol + 1), kv_spec(nxt, kcol + 1),
                  pl.BlockSpec((ctx_len, KV_DIM), lambda b, s: (ctx_blk(b, s), kcol)),
                  pl.BlockSpec((ctx_len, KV_DIM), lambda b, s: (ctx_blk(b, s), kcol + 1))],
        out_specs=pl.BlockSpec((blk, Q_DIM), lambda b, s: (own(b, s), 0)),
        out_shape=jax.ShapeDtypeStruct((n, Q_DIM), BF16),
        compiler_params=_params("parallel", "parallel"),
        name="window_attn",
    )(sink, qkv, qkv, qkv, qkv, qkv, qkv, qkv, qkv, qkv)


def _s5_tables(a_re, a_im, log_step, b_re, b_im, c_re, c_im):
    t = S5_T
    step = jnp.exp(log_step.astype(F32))[..., None]
    a_re = a_re.astype(F32)
    a_im = a_im.astype(F32)
    k = jnp.arange(t + 1, dtype=F32)[:, None, None, None]
    mag = jnp.exp(k * (a_re * step)[None])
    ang = k * (a_im * step)[None]
    pr, pi = mag * jnp.cos(ang), mag * jnp.sin(ang)
    den = a_re * a_re + a_im * a_im
    nr, ni = pr[1] - 1.0, pi[1]
    wr = (nr * a_re + ni * a_im) / den
    wi = (ni * a_re - nr * a_im) / den
    b_re = b_re.astype(F32)[None]
    b_im = b_im.astype(F32)[None]
    wbr = wr[..., None] * b_re - wi[..., None] * b_im
    wbi = wr[..., None] * b_im + wi[..., None] * b_re
    c_re = c_re.astype(F32)[None, None]
    c_im = c_im.astype(F32)[None, None]
    cpr = c_re * pr[:, :, :, None, :] - c_im * pi[:, :, :, None, :]
    cpi = c_re * pi[:, :, :, None, :] + c_im * pr[:, :, :, None, :]
    kern = (jnp.einsum('kdgop,dgpi->kdgoi', cpr, wbr, precision=HIGHEST)
            - jnp.einsum('kdgop,dgpi->kdgoi', cpi, wbi, precision=HIGHEST))
    kf, kb = kern[:t, 0], kern[:t, 1]
    kfull = jnp.concatenate([kb[:0:-1], (kf[0] + kb[0])[None], kf[1:]], axis=0)
    idx = jnp.arange(t)[None, :] - jnp.arange(t)[:, None] + (t - 1)
    m = kfull[idx]
    m = jnp.transpose(m, (2, 0, 4, 1, 3)).reshape(S5_GROUPS, t * S5_GROUP, t * S5_GROUP)

    def cmul(xr, xi, yr, yi):
        return xr * yr - xi * yi, xr * yi + xi * yr

    ffr, ffi = cmul(pr[t - 1::-1, 0][..., None][:t], pi[t - 1::-1, 0][..., None][:t], wbr[0][None], wbi[0][None])
    fbr, fbi = cmul(pr[:t, 1][..., None], pi[:t, 1][..., None], wbr[1][None], wbi[1][None])
    f = jnp.concatenate([ffr, fbr, ffi, fbi], axis=2)
    f = jnp.transpose(f, (1, 0, 3, 2)).reshape(S5_GROUPS, t * S5_GROUP, 4 * S5_STATE)
    efr, efi = cpr[1:t + 1, 0], -cpi[1:t + 1, 0]
    ebr, ebi = cpr[t:0:-1, 1], -cpi[t:0:-1, 1]
    e = jnp.concatenate([efr, ebr, efi, ebi], axis=3)
    e = jnp.transpose(e, (1, 3, 0, 2)).reshape(S5_GROUPS, 4 * S5_STATE, t * S5_GROUP)
    dec_re = jnp.concatenate([pr[t, 0], pr[t, 1]], axis=-1)
    dec_im = jnp.concatenate([pi[t, 0], pi[t, 1]], axis=-1)
    return m.astype(BF16), f.astype(BF16), e.astype(BF16), dec_re, dec_im


def _s5_state_kernel(u_ref, f_ref, sre_ref, sim_ref):
    s = jnp.dot(u_ref[...], f_ref[...], preferred_element_type=F32)
    sre_ref[...] = s[:, :2 * S5_STATE]
    sim_ref[...] = s[:, 2 * S5_STATE:]


def _s5_rec_kernel(sre_ref, sim_ref, are_ref, aim_ref, hre_ref, him_ref, *, n_batch, nlat, nctx):
    are = are_ref[...]
    aim = aim_ref[...]
    fwd = lax.broadcasted_iota(jnp.int32, are.shape, 1) < S5_STATE
    half = S5_STATE

    def body(i, carry):
        out = []
        for b in range(n_batch):
            hr, hi = carry[b]
            lat0 = b * nlat
            ctx0 = n_batch * nlat + b * nctx
            in_ctx = i < nctx
            fi = jnp.where(in_ctx, ctx0 + i, lat0 + i - nctx)
            bi = jnp.where(in_ctx, ctx0 + nctx - 1 - i, lat0 + nlat - 1 - (i - nctx))
            hre_ref[fi, :, 0:half] = hr[:, 0:half]
            hre_ref[bi, :, half:] = hr[:, half:]
            him_ref[fi, :, 0:half] = hi[:, 0:half]
            him_ref[bi, :, half:] = hi[:, half:]
            sr = jnp.where(fwd, sre_ref[fi], sre_ref[bi])
            si = jnp.where(fwd, sim_ref[fi], sim_ref[bi])
            out.append((are * hr - aim * hi + sr, are * hi + aim * hr + si))
        return tuple(out)

    zero = jnp.zeros(are.shape, F32)
    lax.fori_loop(0, nlat + nctx, body, tuple((zero, zero) for _ in range(n_batch)))


def _s5_out_kernel(u_ref, hre_ref, him_ref, m_ref, e_ref, y_ref):
    y = jnp.dot(u_ref[...], m_ref[...], preferred_element_type=F32)
    hin = jnp.concatenate([hre_ref[...], him_ref[...]], axis=1).astype(BF16)
    y_ref[...] = y + jnp.dot(hin, e_ref[...], preferred_element_type=F32)


def _s5(u_g, tables, *, n_batch, seq, ctx_len):
    m, f, e, dec_re, dec_im = tables
    g, nc, tw = u_g.shape
    sw = 2 * S5_STATE
    gspec = lambda shape: pl.BlockSpec((None,) + shape, lambda i: (i, 0, 0))
    s_re, s_im = pl.pallas_call(
        _s5_state_kernel,
        grid=(g,),
        in_specs=[gspec((nc, tw)), gspec((tw, 2 * sw))],
        out_specs=[pl.BlockSpec((nc, sw), lambda i: (0, i)), pl.BlockSpec((nc, sw), lambda i: (0, i))],
        out_shape=[jax.ShapeDtypeStruct((nc, g * sw), F32)] * 2,
        compiler_params=_params("parallel"),
        name="s5_chunk_state",
    )(u_g, f)
    gb = SUBLANES
    blk3 = pl.BlockSpec((nc, gb, sw), lambda i: (0, i, 0))
    dspec = pl.BlockSpec((gb, sw), lambda i: (i, 0))
    h_re, h_im = pl.pallas_call(
        functools.partial(_s5_rec_kernel, n_batch=n_batch, nlat=seq // S5_T, nctx=ctx_len // S5_T),
        grid=(g // gb,),
        in_specs=[blk3, blk3, dspec, dspec],
        out_specs=[blk3, blk3],
        out_shape=[jax.ShapeDtypeStruct((nc, g, sw), F32)] * 2,
        compiler_params=_params("parallel"),
        name="s5_chunk_scan",
    )(s_re.reshape(nc, g, sw), s_im.reshape(nc, g, sw), dec_re, dec_im)
    hspec = pl.BlockSpec((nc, sw), lambda i: (0, i))
    return pl.pallas_call(
        _s5_out_kernel,
        grid=(g,),
        in_specs=[gspec((nc, tw)), hspec, hspec, gspec((tw, tw)), gspec((2 * sw, tw))],
        out_specs=gspec((nc, tw)),
        out_shape=jax.ShapeDtypeStruct((g, nc, tw), F32),
        compiler_params=_params("parallel"),
        name="s5_out",
    )(u_g, h_re.reshape(nc, g * sw), h_im.reshape(nc, g * sw), m, e)


def _ssd_kernel(*refs, rev, finish, n_lat_chunks, n_ctx_chunks):
    if finish:
        (xo_ref, xp_ref, xn_ref, dt_ref, z_ref, yb_ref, cw_ref, cb_ref, bias_ref, a_ref, ex_ref, d_ref, ng_ref,
         o_ref, xe_ref, h_ref, yd_ref) = refs
    else:
        (xo_ref, xp_ref, xn_ref, dt_ref, cw_ref, cb_ref, bias_ref, a_ref, ex_ref,
         o_ref, xe_ref, h_ref, yd_ref) = refs
    q = SSD_Q
    halo = SUBLANES
    i = pl.program_id(1)

    @pl.when(i == 0)
    def _():
        h_ref[...] = jnp.zeros_like(h_ref)

    in_ctx = i < n_ctx_chunks
    pos = jnp.where(in_ctx, i, i - n_ctx_chunks)
    length = jnp.where(in_ctx, n_ctx_chunks, n_lat_chunks)
    if rev:
        pos = length - 1 - pos
    has_prev = pos > 0
    has_next = pos < length - 1

    xe_ref[0:halo, :] = xp_ref[...] * has_prev.astype(F32)
    xe_ref[halo:halo + q, :] = xo_ref[...]
    xe_ref[halo + q:, :] = xn_ref[...] * has_next.astype(F32)
    acc = cb_ref[...] + cw_ref[0:1, :] * xe_ref[halo - 2:halo - 2 + q, :]
    for k in range(1, SSD_CONV):
        acc = acc + cw_ref[k:k + 1, :] * xe_ref[halo - 2 + k:halo - 2 + k + q, :]
    xbc = _silu(acc)
    xs = xbc[:, :SSD_DIM]
    bm_f = xbc[:, SSD_DIM:SSD_DIM + SSD_BC]
    bm = bm_f.astype(BF16)
    cm = xbc[:, SSD_DIM + SSD_BC:].astype(BF16)

    raw = dt_ref[...] + bias_ref[...]
    dt = jnp.maximum(raw, 0.0) + jnp.log1p(jnp.exp(-jnp.abs(raw)))
    ad = dt * a_ref[...]
    li = lax.broadcasted_iota(jnp.int32, (q, q), 0)
    si = lax.broadcasted_iota(jnp.int32, (q, q), 1)
    keep = (si >= li) if rev else (si <= li)
    tri = keep.astype(F32)
    a_cs = jnp.dot(tri, ad, precision=HIGHEST, preferred_element_type=F32)
    a_cs_t = a_cs.T
    dt_t = dt.T
    edge = 0 if rev else q - 1
    tot = a_cs[edge:edge + 1, :]
    scal = jnp.concatenate([dt * jnp.exp(tot - a_cs), jnp.exp(a_cs)], axis=0)
    s_hi, s_lo = _split_bf16(scal)
    ex = ex_ref[...]
    scal_x = (jnp.dot(s_hi, ex, preferred_element_type=F32) + jnp.dot(s_lo, ex, preferred_element_type=F32))
    ds_x = scal_x[:q]
    do_x = scal_x[q:]
    tot_x = do_x[edge:edge + 1, :]

    xs_b = xs.astype(BF16)
    xds_b = (xs * ds_x).astype(BF16)
    rep = SSD_HEADS // SSD_GROUPS
    gw = rep * SSD_HEAD_DIM
    lane0 = SSD_HEADS if rev else 0
    nt = (((1,), (1,)), ((), ()))
    y_parts = []
    for g in range(SSD_GROUPS):
        bg = bm[:, g * SSD_STATE:(g + 1) * SSD_STATE]
        cg = cm[:, g * SSD_STATE:(g + 1) * SSD_STATE]
        cb = lax.dot_general(cg, bg, nt, preferred_element_type=F32)
        for r in range(rep):
            hd = g * rep + r
            col = a_cs[:, lane0 + hd:lane0 + hd + 1]
            row = a_cs_t[lane0 + hd:lane0 + hd + 1, :]
            dec = jnp.exp(jnp.where(keep, col - row, -jnp.inf)) * dt_t[lane0 + hd:lane0 + hd + 1, :]
            yd_ref[:, hd * SSD_HEAD_DIM:(hd + 1) * SSD_HEAD_DIM] = jnp.dot(
                (cb * dec).astype(BF16), xs_b[:, hd * SSD_HEAD_DIM:(hd + 1) * SSD_HEAD_DIM],
                preferred_element_type=F32)
        gs = slice(g * gw, (g + 1) * gw)
        h_g = h_ref[:, gs]
        y_off = jnp.dot(cg, h_g.astype(BF16), preferred_element_type=F32) * do_x[:, gs]
        bg_t = bm_f[:, g * SSD_STATE:(g + 1) * SSD_STATE].T.astype(BF16)
        st = jnp.dot(bg_t, xds_b[:, gs], preferred_element_type=F32)
        h_ref[:, gs] = h_g * tot_x[:, gs] + st
        y_parts.append(y_off)
    y = yd_ref[...] + jnp.concatenate(y_parts, axis=1)

    if finish:
        y = y + yb_ref[...] + d_ref[...] * xs
        y = y * _silu(z_ref[...])
        nw = SSD_DIM // SSD_GROUPS
        outs = []
        for g in range(SSD_GROUPS):
            yg = y[:, g * nw:(g + 1) * nw]
            outs.append(yg * lax.rsqrt(jnp.mean(yg * yg, axis=-1, keepdims=True) + NORM_EPS))
        o_ref[...] = (jnp.concatenate(outs, axis=1) * ng_ref[...]).astype(BF16)
    else:
        o_ref[...] = y


def _ssd_pass(p, y_other, vecs, *, rev, n_batch, seq, ctx_len, col_dt):
    n = p.shape[0]
    q = SSD_Q
    halo = SUBLANES
    nl, ncx = seq // q, ctx_len // q
    hpq = q // halo
    finish = y_other is not None
    conv_w, conv_b, bias, a_vec, expand, d_vec, norm_g = vecs

    def chunk(b, i):
        pos_c = (ncx - 1 - i) if rev else i
        pos_l = (nl - 1 - (i - ncx)) if rev else (i - ncx)
        return jnp.where(i < ncx, n_batch * nl + b * ncx + pos_c, b * nl + pos_l)

    last_halo = n // halo - 1
    own = lambda col_blk, w: pl.BlockSpec((q, w), lambda b, i: (chunk(b, i), col_blk))
    prev = pl.BlockSpec((halo, SSD_XBC), lambda b, i: (jnp.maximum(chunk(b, i) * hpq - 1, 0), COL_XBC // SSD_XBC))
    nxt = pl.BlockSpec((halo, SSD_XBC),
                       lambda b, i: (jnp.minimum((chunk(b, i) + 1) * hpq, last_halo), COL_XBC // SSD_XBC))
    const = lambda a: pl.BlockSpec(a.shape, lambda b, i: (0, 0))
    in_specs = [own(COL_XBC // SSD_XBC, SSD_XBC), prev, nxt, own(col_dt // LANES, LANES)]
    args = [p, p, p, p]
    if finish:
        in_specs += [own(COL_Z // SSD_DIM, SSD_DIM), own(0, SSD_DIM)]
        args += [p, y_other]
    tail = [conv_w, conv_b, bias, a_vec, expand] + ([d_vec, norm_g] if finish else [])
    in_specs += [const(a) for a in tail]
    args += tail
    return pl.pallas_call(
        functools.partial(_ssd_kernel, rev=rev, finish=finish, n_lat_chunks=nl, n_ctx_chunks=ncx),
        grid=(n_batch, nl + ncx),
        in_specs=in_specs,
        out_specs=own(0, SSD_DIM),
        out_shape=jax.ShapeDtypeStruct((n, SSD_DIM), BF16 if finish else F32),
        scratch_shapes=[pltpu.VMEM((q + 2 * halo, SSD_XBC), F32),
                        pltpu.VMEM((SSD_STATE, SSD_DIM), F32),
                        pltpu.VMEM((q, SSD_DIM), F32)],
        compiler_params=_params("parallel", "arbitrary"),
        name="ssd_bwd" if rev else "ssd_fwd",
    )(*args)


def _ssd(p, conv_w, conv_b, dt_bias, a_log, d_skip, norm_g, *, n_batch, seq, ctx_len, col_dt):
    cw = jnp.zeros((SUBLANES, SSD_XBC), F32).at[:SSD_CONV].set(conv_w.astype(F32))
    cb = conv_b.astype(F32).reshape(1, SSD_XBC)
    bias = jnp.zeros((1, LANES), F32).at[0, :2 * SSD_HEADS].set(dt_bias.astype(F32).reshape(-1))
    a = -jnp.exp(a_log.astype(F32))
    d_vec = jnp.repeat(d_skip.astype(F32), SSD_HEAD_DIM).reshape(1, SSD_DIM)
    ng = norm_g.astype(F32).reshape(1, SSD_DIM)
    head_of_channel = jnp.arange(SSD_DIM) // SSD_HEAD_DIM
    outs = {}
    for rev in (True, False):
        lane0 = SSD_HEADS if rev else 0
        a_vec = jnp.zeros((1, LANES), F32).at[0, lane0:lane0 + SSD_HEADS].set(a[1 if rev else 0])
        expand = (jnp.arange(LANES)[:, None] == (head_of_channel[None, :] + lane0)).astype(BF16)
        outs[rev] = _ssd_pass(p, None if rev else outs[True], (cw, cb, bias, a_vec, expand, d_vec, ng),
                              rev=rev, n_batch=n_batch, seq=seq, ctx_len=ctx_len, col_dt=col_dt)
    return outs[False]


def _s5_glu_kernel(y_ref, u_ref, d_ref, w_ref, o_ref):
    y = y_ref[...] + d_ref[...] * u_ref[...]
    g = 0.5 * y * (1.0 + jnp.tanh(np.sqrt(2.0 / np.pi).astype(np.float32) * (y + 0.044715 * (y * y * y))))
    gate = _sigmoid(jnp.dot(g.astype(BF16), w_ref[...], preferred_element_type=F32))
    o_ref[...] = (g * gate).astype(BF16)


def _s5_glu(ys5, p, d_s5, w_glu, *, n_rows):
    tm = ROW_TILE
    return pl.pallas_call(
        _s5_glu_kernel,
        grid=(n_rows // tm,),
        in_specs=[pl.BlockSpec((tm, S5_DIM), lambda i: (i, 0)),
                  pl.BlockSpec((tm, S5_DIM), lambda i: (i, COL_S5 // S5_DIM)),
                  pl.BlockSpec((1, S5_DIM), lambda i: (0, 0)),
                  pl.BlockSpec((S5_DIM, S5_DIM), lambda i: (0, 0))],
        out_specs=pl.BlockSpec((tm, S5_DIM), lambda i: (i, 0)),
        out_shape=jax.ShapeDtypeStruct((n_rows, S5_DIM), BF16),
        compiler_params=_params("parallel"),
        name="s5_glu",
    )(ys5, p, d_s5, w_glu)


def _merge_kernel(h_ref, mod_ref, oa_ref, os_ref, od_ref, g0_ref, g1_ref, g2_ref,
                  wa_ref, ws_ref, wd_ref, wo_ref, o_ref, acc_ref):
    j = pl.program_id(1)

    @pl.when(j == 0)
    def _():
        acc_ref[...] = jnp.zeros_like(acc_ref)

    m = (_sigmoid(g0_ref[...]) * jnp.dot(oa_ref[...], wa_ref[...], preferred_element_type=F32)
         + _sigmoid(g1_ref[...]) * jnp.dot(os_ref[...], ws_ref[...], preferred_element_type=F32)
         + _sigmoid(g2_ref[...]) * jnp.dot(od_ref[...], wd_ref[...], preferred_element_type=F32))
    acc_ref[...] += jnp.dot(m.astype(BF16), wo_ref[...], preferred_element_type=F32)

    @pl.when(j == pl.num_programs(1) - 1)
    def _():
        o_ref[...] = h_ref[...] + mod_ref[5:6, :] * acc_ref[...]


def _merge(h, mod, p, oa, os5, ossd, w_a, w_s, w_d, w_o, *, n_rows, tiles_per_batch, n_batch):
    d = h.shape[1]
    tm, tc = ROW_TILE, 512
    nj = d // tc
    midx = lambda i, j: (jnp.minimum(i // tiles_per_batch, n_batch), 0, 0)
    row = lambda w: pl.BlockSpec((tm, w), lambda i, j: (i, 0))
    gate = lambda br: pl.BlockSpec((tm, tc), lambda i, j: (i, (COL_GATE + br * d) // tc + j))
    wcol = lambda k: pl.BlockSpec((k, tc), lambda i, j: (0, j))
    return pl.pallas_call(
        _merge_kernel,
        grid=(n_rows // tm, nj),
        in_specs=[row(d), pl.BlockSpec((None, N_MOD, d), midx),
                  row(Q_DIM), row(S5_DIM), row(SSD_DIM), gate(0), gate(1), gate(2),
                  wcol(Q_DIM), wcol(S5_DIM), wcol(SSD_DIM),
                  pl.BlockSpec((tc, d), lambda i, j: (j, 0))],
        out_specs=row(d),
        out_shape=jax.ShapeDtypeStruct((n_rows, d), F32),
        scratch_shapes=[pltpu.VMEM((tm, d), F32)],
        compiler_params=_params("parallel", "arbitrary"),
        name="merge",
    )(h, mod, oa, os5, ossd, p, p, p, w_a, w_s, w_d, w_o)


def _rope_tables(seq, extra_rows):
    rows = seq // GRID_W
    row = jnp.repeat(jnp.arange(rows, dtype=F32), GRID_W)
    col = jnp.tile(jnp.arange(GRID_W, dtype=F32), rows)
    axis_dim = HEAD_DIM // 2
    inv_freq = ROPE_BASE ** (-jnp.arange(0, axis_dim, 2, dtype=F32) / axis_dim)
    ang_r = row[:, None] * inv_freq[None, :]
    ang_c = col[:, None] * inv_freq[None, :]
    cos = jnp.concatenate([jnp.cos(ang_r), jnp.cos(ang_c)] * 2, axis=1)
    sin = jnp.concatenate([-jnp.sin(ang_r), -jnp.sin(ang_c), jnp.sin(ang_r), jnp.sin(ang_c)], axis=1)
    cos = jnp.concatenate([cos, jnp.ones((extra_rows, HEAD_DIM), F32)], axis=0)
    sin = jnp.concatenate([sin, jnp.zeros((extra_rows, HEAD_DIM), F32)], axis=0)
    return cos, sin


def _pair_major(t):
    lead = t.shape[:-1]
    t = t.reshape(lead + (-1, 2, 2, HEAD_DIM // 4))
    return jnp.swapaxes(t, -3, -2).reshape(lead + (-1,))


def _permute_w_in(w, d):
    q0, k0, v0 = 0, Q_DIM, Q_DIM + KV_DIM
    w = jnp.concatenate([_pair_major(w[:, :v0]), w[:, v0:]], axis=1)
    s0 = Q_DIM + 2 * KV_DIM
    z0 = s0 + S5_DIM
    x0 = z0 + SSD_DIM
    t0 = x0 + SSD_XBC
    g0 = t0 + 2 * SSD_HEADS
    g1 = g0 + N_BRANCH * d
    used = COL_GATE + N_BRANCH * d + 2 * SSD_HEADS
    total = -(-used // PROJ_TN) * PROJ_TN
    parts = [w[:, q0:s0], w[:, x0:t0], w[:, s0:z0], w[:, z0:x0], w[:, g0:g1], w[:, t0:g0],
             jnp.zeros((w.shape[0], total - used), w.dtype)]
    return jnp.concatenate(parts, axis=1).astype(BF16)


def kernel(x, c, ctx, c_ctx, w_ada, b_ada, norm_g, w_ffn_gate, w_ffn_up, w_ffn_down, w_in, qk_norm_g, attn_sink,
           w_attn_o, s5_a_re, s5_a_im, s5_log_step, s5_b_re, s5_b_im, s5_c_re, s5_c_im, s5_d, s5_w_glu, w_s5_o,
           ssd_conv_w, ssd_conv_b, ssd_dt_bias, ssd_a_log, ssd_d, ssd_norm_g, w_ssd_o, w_out):
    n_batch, seq, d = x.shape
    ctx_len = ctx.shape[1]
    depth = w_ada.shape[0]
    n_lat, n_ctx = n_batch * seq, n_batch * ctx_len
    n = n_lat + n_ctx
    assert seq % ROW_TILE == 0 and n_ctx % ROW_TILE == 0 and ctx_len % SSD_Q == 0 and d % PROJ_TN == 0
    assert n_batch + 1 <= SUBLANES
    col_dt = COL_GATE + N_BRANCH * d

    h = jnp.concatenate([x.reshape(n_lat, d), ctx.reshape(n_ctx, d)], axis=0)
    c_rows = jnp.zeros((SUBLANES, d), F32).at[:n_batch].set(c.astype(F32)).at[n_batch].set(c_ctx.astype(F32))
    mods = _ada(c_rows, w_ada, b_ada)[:, :n_batch + 1].reshape(depth, n_batch + 1, N_MOD, d)
    cos_t, sin_t = _rope_tables(seq, ROW_TILE)

    tpb = seq // ROW_TILE
    for l in range(depth):
        last = l == depth - 1
        ffn_w = lambda k: (w_ffn_gate[l, k].astype(BF16), w_ffn_up[l, k].astype(BF16), w_ffn_down[l, k].astype(BF16))
        g = norm_g[l].astype(F32)
        h = _ffn(h, mods[l], g, *ffn_w(0), rows=(0, 1, 2, 0), n_rows=n, tiles_per_batch=tpb, n_batch=n_batch)
        p = _proj(h, mods[l], g, _permute_w_in(w_in[l], d), tiles_per_batch=tpb, n_batch=n_batch)

        qkv = _prep(p, cos_t, sin_t, _pair_major(qk_norm_g[l].astype(F32)), tiles_lat=n_lat // ROW_TILE,
                    tiles_per_batch=tpb)
        oa = _attention(qkv, attn_sink[l].astype(F32), n_batch=n_batch, seq=seq, ctx_len=ctx_len, with_ctx=not last)

        nc = n // S5_T
        u_g = p[:, COL_S5:COL_S5 + S5_DIM].reshape(nc, S5_T, S5_GROUPS, S5_GROUP)
        u_g = jnp.transpose(u_g, (2, 0, 1, 3)).reshape(S5_GROUPS, nc, S5_T * S5_GROUP).astype(BF16)
        tables = _s5_tables(s5_a_re[l], s5_a_im[l], s5_log_step[l], s5_b_re[l], s5_b_im[l], s5_c_re[l], s5_c_im[l])
        y_g = _s5(u_g, tables, n_batch=n_batch, seq=seq, ctx_len=ctx_len)
        ys5 = jnp.transpose(y_g.reshape(S5_GROUPS, nc, S5_T, S5_GROUP), (1, 2, 0, 3)).reshape(n, S5_DIM)

        ossd = _ssd(p, ssd_conv_w[l], ssd_conv_b[l], ssd_dt_bias[l], ssd_a_log[l], ssd_d[l], ssd_norm_g[l],
                    n_batch=n_batch, seq=seq, ctx_len=ctx_len, col_dt=col_dt)

        n_out = n_lat if last else n
        os5 = _s5_glu(ys5, p, s5_d[l].astype(F32).reshape(1, S5_DIM), s5_w_glu[l].astype(BF16), n_rows=n_out)
        h = _merge(h, mods[l], p, oa, os5, ossd,
                   w_attn_o[l].astype(BF16), w_s5_o[l].astype(BF16), w_ssd_o[l].astype(BF16), w_out[l].astype(BF16),
                   n_rows=n_out, tiles_per_batch=tpb, n_batch=n_batch)
        h = _ffn(h, mods[l], g, *ffn_w(1), rows=(6, 7, 8, 2), n_rows=n_out, tiles_per_batch=tpb, n_batch=n_batch)
    return h.reshape(n_batch, seq, d)
```

```python
import functools

import numpy as np
import jax
import jax.numpy as jnp
from jax import lax
from jax.experimental import pallas as pl
from jax.experimental.pallas import tpu as pltpu

F32 = jnp.float32
BF16 = jnp.bfloat16
HIGHEST = lax.Precision.HIGHEST

GRID_W = 64
N_MOD = 9
NORM_EPS = 1e-6
ATTN_HEADS = 12
ATTN_KV_HEADS = 4
HEAD_DIM = 128
ATTN_BLOCK = 128
ROPE_BASE = 10000.0
S5_DIM = 1024
S5_GROUP = 16
S5_GROUPS = S5_DIM // S5_GROUP
S5_STATE = 64
SSD_DIM = 1536
SSD_HEAD_DIM = 64
SSD_HEADS = SSD_DIM // SSD_HEAD_DIM
SSD_GROUPS = 4
SSD_STATE = 128
SSD_CONV = 5
N_BRANCH = 3
Q_DIM = ATTN_HEADS * HEAD_DIM
KV_DIM = ATTN_KV_HEADS * HEAD_DIM
SSD_BC = SSD_GROUPS * SSD_STATE
SSD_XBC = SSD_DIM + 2 * SSD_BC

VMEM_LIMIT_BYTES = 56 * 1024 * 1024
LANES = 128
SUBLANES = 8

COL_QKV = 0
COL_XBC = Q_DIM + 2 * KV_DIM
COL_DT = COL_XBC + SSD_XBC
DT_SLOT = 1024
COL_Z = COL_DT + DT_SLOT
COL_GATE = COL_Z + SSD_DIM
PROJ_TN = 1536

ROW_TILE = 512
S5_T = 128
SSD_Q = 128


def _params(*sem):
    return pltpu.CompilerParams(dimension_semantics=sem, vmem_limit_bytes=VMEM_LIMIT_BYTES)


def _sigmoid(v):
    return 0.5 * (1.0 + jnp.tanh(0.5 * v))


def _silu(v):
    return v * _sigmoid(v)


def _split_bf16(v):
    hi = v.astype(BF16)
    return hi, (v - hi.astype(F32)).astype(BF16)


def _rms_mod(x, g, shift, scale):
    y = x * lax.rsqrt(jnp.mean(x * x, axis=-1, keepdims=True) + NORM_EPS) * g
    return y * (1.0 + scale) + shift


def _ada_kernel(c_ref, w_ref, b_ref, o_ref):
    s = _silu(c_ref[...]).astype(BF16)
    o_ref[...] = jnp.dot(s, w_ref[...].astype(BF16), preferred_element_type=F32) + b_ref[...]


def _ada(c_rows, w_ada, b_ada):
    depth, d, n = w_ada.shape
    tn = 1024
    return pl.pallas_call(
        _ada_kernel,
        grid=(depth, n // tn),
        in_specs=[pl.BlockSpec((SUBLANES, d), lambda l, j: (0, 0)),
                  pl.BlockSpec((None, d, tn), lambda l, j: (l, 0, j)),
                  pl.BlockSpec((None, 1, tn), lambda l, j: (l, 0, j))],
        out_specs=pl.BlockSpec((None, SUBLANES, tn), lambda l, j: (l, 0, j)),
        out_shape=jax.ShapeDtypeStruct((depth, SUBLANES, n), F32),
        compiler_params=_params("parallel", "parallel"),
        name="ada_mod",
    )(c_rows, w_ada, b_ada.reshape(depth, 1, n))


def _ffn_kernel(h_ref, mod_ref, g_ref, wg_ref, wu_ref, wd_ref, o_ref, xn_ref, acc_ref, *, rows):
    r_shift, r_scale, r_gate, r_norm = rows
    j = pl.program_id(1)

    @pl.when(j == 0)
    def _():
        xn = _rms_mod(h_ref[...], g_ref[r_norm:r_norm + 1, :], mod_ref[r_shift:r_shift + 1, :],
                      mod_ref[r_scale:r_scale + 1, :])
        xn_ref[...] = xn.astype(BF16)
        acc_ref[...] = jnp.zeros_like(acc_ref)

    xn = xn_ref[...]
    a = jnp.dot(xn, wg_ref[...], preferred_element_type=F32)
    b = jnp.dot(xn, wu_ref[...], preferred_element_type=F32)
    t = (_silu(a) * b).astype(BF16)
    acc_ref[...] += jnp.dot(t, wd_ref[...], preferred_element_type=F32)

    @pl.when(j == pl.num_programs(1) - 1)
    def _():
        o_ref[...] = h_ref[...] + 0.5 * mod_ref[r_gate:r_gate + 1, :] * acc_ref[...]


def _ffn(h, mod, g, wg, wu, wd, *, rows, n_rows, tiles_per_batch, n_batch):
    d = h.shape[1]
    f = wg.shape[1]
    tm, tf = ROW_TILE, 512
    midx = lambda i, j: (jnp.minimum(i // tiles_per_batch, n_batch), 0, 0)
    return pl.pallas_call(
        functools.partial(_ffn_kernel, rows=rows),
        grid=(n_rows // tm, f // tf),
        in_specs=[pl.BlockSpec((tm, d), lambda i, j: (i, 0)),
                  pl.BlockSpec((None, N_MOD, d), midx),
                  pl.BlockSpec(g.shape, lambda i, j: (0, 0)),
                  pl.BlockSpec((d, tf), lambda i, j: (0, j)),
                  pl.BlockSpec((d, tf), lambda i, j: (0, j)),
                  pl.BlockSpec((tf, d), lambda i, j: (j, 0))],
        out_specs=pl.BlockSpec((tm, d), lambda i, j: (i, 0)),
        out_shape=jax.ShapeDtypeStruct((n_rows, d), F32),
        scratch_shapes=[pltpu.VMEM((tm, d), BF16), pltpu.VMEM((tm, d), F32)],
        compiler_params=_params("parallel", "arbitrary"),
        name="ffn",
    )(h, mod, g, wg, wu, wd)


def _proj_kernel(h_ref, mod_ref, g_ref, w_ref, ws_ref, o_ref, ut_ref, u_ref):
    @pl.when(pl.program_id(1) == 0)
    def _():
        u = _rms_mod(h_ref[...], g_ref[1:2, :], mod_ref[3:4, :], mod_ref[4:5, :]).astype(BF16)
        u_ref[...] = u
        ut_ref[...] = lax.dot_general(ws_ref[...], u, (((1,), (1,)), ((), ())), preferred_element_type=F32)

    o_ref[...] = jnp.dot(u_ref[...], w_ref[...], preferred_element_type=F32)


def _proj(h, mod, g, w, ws_t, *, tiles_per_batch, n_batch):
    n, d = h.shape
    ncol = w.shape[1]
    tm, tn = ROW_TILE, PROJ_TN
    midx = lambda i, j: (jnp.minimum(i // tiles_per_batch, n_batch), 0, 0)
    return pl.pallas_call(
        _proj_kernel,
        grid=(n // tm, ncol // tn),
        in_specs=[pl.BlockSpec((tm, d), lambda i, j: (i, 0)),
                  pl.BlockSpec((None, N_MOD, d), midx),
                  pl.BlockSpec(g.shape, lambda i, j: (0, 0)),
                  pl.BlockSpec((d, tn), lambda i, j: (0, j)),
                  pl.BlockSpec(ws_t.shape, lambda i, j: (0, 0))],
        out_specs=[pl.BlockSpec((tm, tn), lambda i, j: (i, j)),
                   pl.BlockSpec((S5_DIM, tm), lambda i, j: (0, i))],
        out_shape=[jax.ShapeDtypeStruct((n, ncol), F32), jax.ShapeDtypeStruct((S5_DIM, n), F32)],
        scratch_shapes=[pltpu.VMEM((tm, d), BF16)],
        compiler_params=_params("parallel", "arbitrary"),
        name="in_proj",
    )(h, mod, g, w, ws_t)


def _prep_kernel(p_ref, cos_ref, sin_ref, g_ref, o_ref):
    cos = cos_ref[...]
    sin = sin_ref[...]
    scale = HEAD_DIM ** -0.5
    mean_mat = jnp.full((HEAD_DIM, HEAD_DIM), 1.0 / HEAD_DIM, BF16)
    for hd in range(ATTN_HEADS + ATTN_KV_HEADS):
        sl = slice(hd * HEAD_DIM, (hd + 1) * HEAD_DIM)
        x = p_ref[:, sl]
        g = g_ref[0:1, :] if hd < ATTN_HEADS else g_ref[1:2, :]
        sq_hi, sq_lo = _split_bf16(x * x)
        ms = (jnp.dot(sq_hi, mean_mat, preferred_element_type=F32)
              + jnp.dot(sq_lo, mean_mat, preferred_element_type=F32))
        y = x * lax.rsqrt(ms + NORM_EPS) * g
        y = y * cos + pltpu.roll(y, HEAD_DIM // 2, 1) * sin
        if hd < ATTN_HEADS:
            y = y * scale
        o_ref[:, sl] = y.astype(BF16)
    vs = slice(Q_DIM + KV_DIM, Q_DIM + 2 * KV_DIM)
    o_ref[:, vs] = p_ref[:, vs].astype(BF16)


def _prep(p, cos_t, sin_t, qk_g, *, tiles_lat, tiles_per_batch):
    n = p.shape[0]
    tm = ROW_TILE
    w = Q_DIM + 2 * KV_DIM
    tidx = lambda i: (jnp.where(i < tiles_lat, i % tiles_per_batch, tiles_per_batch), 0)
    return pl.pallas_call(
        _prep_kernel,
        grid=(n // tm,),
        in_specs=[pl.BlockSpec((tm, w), lambda i: (i, 0)),
                  pl.BlockSpec((tm, HEAD_DIM), tidx),
                  pl.BlockSpec((tm, HEAD_DIM), tidx),
                  pl.BlockSpec(qk_g.shape, lambda i: (0, 0))],
        out_specs=pl.BlockSpec((tm, w), lambda i: (i, 0)),
        out_shape=jax.ShapeDtypeStruct((n, w), BF16),
        compiler_params=_params("parallel"),
        name="qk_prep",
    )(p, cos_t, sin_t, qk_g)


def _attn_kernel(sink_ref, q_ref, kp_ref, ko_ref, kn_ref, vp_ref, vo_ref, vn_ref, kc_ref, vc_ref, o_ref, *, nb):
    n = pl.program_id(1)
    blk = ATTN_BLOCK
    grp = ATTN_HEADS // ATTN_KV_HEADS
    is_lat = n < nb
    has_prev = jnp.logical_and(is_lat, n > 0)
    has_next = jnp.logical_and(is_lat, n < nb - 1)
    rows = grp * blk
    r = lax.broadcasted_iota(jnp.int32, (rows, 3 * blk), 0)
    kk = lax.broadcasted_iota(jnp.int32, (rows, 3 * blk), 1)
    i = r & (blk - 1)
    far = 4 * blk
    off_prev = jnp.where(has_prev, 0, far)
    off_own = jnp.where(is_lat, 0, far)
    off_next = jnp.where(has_next, 0, far)
    mask = (((kk < blk) & (kk >= i + off_prev))
            | ((kk >= blk + off_own) & (kk < 2 * blk))
            | ((kk >= 2 * blk) & (kk - 2 * blk <= i - off_next)))
    rr = lax.broadcasted_iota(jnp.int32, (rows, 1), 0)
    nt = (((1,), (1,)), ((), ()))
    for kv in range(ATTN_KV_HEADS):
        ks = slice(kv * HEAD_DIM, (kv + 1) * HEAD_DIM)
        qh = jnp.concatenate([q_ref[:, (kv * grp + g) * HEAD_DIM:(kv * grp + g + 1) * HEAD_DIM] for g in range(grp)],
                             axis=0)
        kl = jnp.concatenate([kp_ref[:, ks], ko_ref[:, ks], kn_ref[:, ks]], axis=0)
        vl = jnp.concatenate([vp_ref[:, ks], vo_ref[:, ks], vn_ref[:, ks]], axis=0)
        s_loc = lax.dot_general(qh, kl, nt, preferred_element_type=F32)
        s_ctx = lax.dot_general(qh, kc_ref[:, ks], nt, preferred_element_type=F32)
        s_loc = jnp.where(mask, s_loc, -jnp.inf)
        sink = jnp.full((rows, 1), sink_ref[kv * grp], F32)
        for g in range(1, grp):
            sink = jnp.where(rr >= g * blk, sink_ref[kv * grp + g], sink)
        m = jnp.maximum(jnp.maximum(jnp.max(s_loc, axis=-1, keepdims=True),
                                    jnp.max(s_ctx, axis=-1, keepdims=True)), sink)
        p_loc = jnp.exp(s_loc - m)
        p_ctx = jnp.exp(s_ctx - m)
        denom = (jnp.sum(p_loc, axis=-1, keepdims=True) + jnp.sum(p_ctx, axis=-1, keepdims=True)
                 + jnp.exp(sink - m))
        o = (jnp.dot(p_loc.astype(BF16), vl, preferred_element_type=F32)
             + jnp.dot(p_ctx.astype(BF16), vc_ref[:, ks], preferred_element_type=F32))
        o = o * (1.0 / denom)
        for g in range(grp):
            hs = slice((kv * grp + g) * HEAD_DIM, (kv * grp + g + 1) * HEAD_DIM)
            o_ref[:, hs] = o[g * blk:(g + 1) * blk, :].astype(BF16)


def _attention(qkv, sink, *, n_batch, seq, ctx_len, with_ctx):
    n = qkv.shape[0]
    blk = ATTN_BLOCK
    nb = seq // blk
    ncb = ctx_len // blk
    steps = nb + (ncb if with_ctx else 0)
    ctx0 = n_batch * nb
    kcol = Q_DIM // KV_DIM

    def own(b, s):
        return jnp.where(s < nb, b * nb + s, ctx0 + b * ncb + (s - nb))

    def prev(b, s):
        return jnp.where(s < nb, b * nb + jnp.maximum(s - 1, 0), ctx0 + b * ncb)

    def nxt(b, s):
        return jnp.where(s < nb, b * nb + jnp.minimum(s + 1, nb - 1), ctx0 + b * ncb)

    ctx_blk = lambda b, s: ((n_batch * seq) // ctx_len + b)
    kv_spec = lambda f, col: pl.BlockSpec((blk, KV_DIM), lambda b, s: (f(b, s), col))
    return pl.pallas_call(
        functools.partial(_attn_kernel, nb=nb),
        grid=(n_batch, steps),
        in_specs=[pl.BlockSpec(memory_space=pltpu.SMEM),
                  pl.BlockSpec((blk, Q_DIM), lambda b, s: (own(b, s), 0)),
                  kv_spec(prev, kcol), kv_spec(own, kcol), kv_spec(nxt, kcol),
                  kv_spec(prev, kcol + 1), kv_spec(own, kcol + 1), kv_spec(nxt, kcol + 1),
                  pl.BlockSpec((ctx_len, KV_DIM), lambda b, s: (ctx_blk(b, s), kcol)),
                  pl.BlockSpec((ctx_len, KV_DIM), lambda b, s: (ctx_blk(b, s), kcol + 1))],
        out_specs=pl.BlockSpec((blk, Q_DIM), lambda b, s: (own(b, s), 0)),
        out_shape=jax.ShapeDtypeStruct((n, Q_DIM), BF16),
        compiler_params=_params("parallel", "parallel"),
        name="window_attn",
    )(sink, qkv, qkv, qkv, qkv, qkv, qkv, qkv, qkv, qkv)


def _s5_tables(a_re, a_im, log_step, b_re, b_im, c_re, c_im):
    t = S5_T
    step = jnp.exp(log_step.astype(F32))[..., None]
    a_re = a_re.astype(F32)
    a_im = a_im.astype(F32)
    k = jnp.arange(t + 1, dtype=F32)[:, None, None, None]
    mag = jnp.exp(k * (a_re * step)[None])
    ang = k * (a_im * step)[None]
    pr, pi = mag * jnp.cos(ang), mag * jnp.sin(ang)
    den = a_re * a_re + a_im * a_im
    nr, ni = pr[1] - 1.0, pi[1]
    wr = (nr * a_re + ni * a_im) / den
    wi = (ni * a_re - nr * a_im) / den
    b_re = b_re.astype(F32)[None]
    b_im = b_im.astype(F32)[None]
    wbr = wr[..., None] * b_re - wi[..., None] * b_im
    wbi = wr[..., None] * b_im + wi[..., None] * b_re
    c_re = c_re.astype(F32)
    c_im = c_im.astype(F32)
    grp = S5_GROUP

    wbr_t = jnp.swapaxes(wbr, 2, 3)[:, :, :, None, :]
    wbi_t = jnp.swapaxes(wbi, 2, 3)[:, :, :, None, :]
    qr = (c_re[None, :, None] * wbr_t - c_im[None, :, None] * wbi_t).reshape(2, S5_GROUPS, grp * grp, S5_STATE)
    qi = (c_re[None, :, None] * wbi_t + c_im[None, :, None] * wbr_t).reshape(2, S5_GROUPS, grp * grp, S5_STATE)

    def taps(d, pw_r, pw_i):
        return (jnp.einsum('gmp,kgp->gmk', qr[d], pw_r, precision=HIGHEST)
                - jnp.einsum('gmp,kgp->gmk', qi[d], pw_i, precision=HIGHEST))

    kf = taps(0, pr[:t, 0], pi[:t, 0])
    kf = kf.at[:, :, 0].add(jnp.sum(qr[1], axis=-1))
    kb = taps(1, pr[t:0:-1, 1], pi[t:0:-1, 1])

    def cmul(xr, xi, yr, yi):
        return xr * yr - xi * yi, xr * yi + xi * yr

    def rows_in_time(pw):
        return jnp.transpose(pw, (1, 0, 2))[:, None]

    def wb_of(d, w):
        return jnp.swapaxes(w[d], 1, 2)[:, :, None, :]

    ffr, ffi = cmul(rows_in_time(pr[t - 1::-1, 0]), rows_in_time(pi[t - 1::-1, 0]), wb_of(0, wbr), wb_of(0, wbi))
    fbr, fbi = cmul(rows_in_time(pr[:t, 1]), rows_in_time(pi[:t, 1]), wb_of(1, wbr), wb_of(1, wbi))
    f = jnp.concatenate([ffr, fbr, ffi, fbi], axis=-1).reshape(S5_GROUPS, grp * t, 4 * S5_STATE)

    def state_rows(pw_r, pw_i):
        pw_r = jnp.transpose(pw_r, (1, 2, 0))[:, :, None, :]
        pw_i = jnp.transpose(pw_i, (1, 2, 0))[:, :, None, :]
        cr = jnp.swapaxes(c_re, 1, 2)[..., None]
        ci = jnp.swapaxes(c_im, 1, 2)[..., None]
        return cr * pw_r - ci * pw_i, -(cr * pw_i + ci * pw_r)

    efr, efi = state_rows(pr[1:t + 1, 0], pi[1:t + 1, 0])
    ebr, ebi = state_rows(pr[t:0:-1, 1], pi[t:0:-1, 1])
    e = jnp.concatenate([efr, ebr, efi, ebi], axis=1).reshape(S5_GROUPS, 4 * S5_STATE, grp * t)
    dec_re = jnp.concatenate([pr[t, 0], pr[t, 1]], axis=-1)
    dec_im = jnp.concatenate([pi[t, 0], pi[t, 1]], axis=-1)
    return kf, kb, f.astype(BF16), e.astype(BF16), dec_re, dec_im


def _s5_chunk_rows(u_ref):
    return jnp.concatenate([u_ref[i] for i in range(S5_GROUP)], axis=1).astype(BF16)


def _s5_state_kernel(u_ref, f_ref, sre_ref, sim_ref):
    s = jnp.dot(_s5_chunk_rows(u_ref), f_ref[...], preferred_element_type=F32)
    sre_ref[...] = s[:, :2 * S5_STATE]
    sim_ref[...] = s[:, 2 * S5_STATE:]


def _s5_rec_kernel(sre_ref, sim_ref, are_ref, aim_ref, hre_ref, him_ref, *, n_batch, nlat, nctx):
    are = are_ref[...]
    aim = aim_ref[...]
    fwd = lax.broadcasted_iota(jnp.int32, are.shape, 1) < S5_STATE
    half = S5_STATE

    def body(i, carry):
        out = []
        for b in range(n_batch):
            hr, hi = carry[b]
            lat0 = b * nlat
            ctx0 = n_batch * nlat + b * nctx
            in_ctx = i < nctx
            fi = jnp.where(in_ctx, ctx0 + i, lat0 + i - nctx)
            bi = jnp.where(in_ctx, ctx0 + nctx - 1 - i, lat0 + nlat - 1 - (i - nctx))
            hre_ref[fi, :, 0:half] = hr[:, 0:half]
            hre_ref[bi, :, half:] = hr[:, half:]
            him_ref[fi, :, 0:half] = hi[:, 0:half]
            him_ref[bi, :, half:] = hi[:, half:]
            sr = jnp.where(fwd, sre_ref[fi], sre_ref[bi])
            si = jnp.where(fwd, sim_ref[fi], sim_ref[bi])
            out.append((are * hr - aim * hi + sr, are * hi + aim * hr + si))
        return tuple(out)

    zero = jnp.zeros(are.shape, F32)
    lax.fori_loop(0, nlat + nctx, body, tuple((zero, zero) for _ in range(n_batch)))


def _s5_out_kernel(u_ref, hre_ref, him_ref, kf_ref, kb_ref, e_ref, y_ref, m_ref):
    t = S5_T
    s_idx = lax.broadcasted_iota(jnp.int32, (t, t), 0)
    t_idx = lax.broadcasted_iota(jnp.int32, (t, t), 1)
    causal = t_idx >= s_idx

    def fill(i, carry):
        r0 = pl.multiple_of(i * S5_GROUP, S5_GROUP)
        kf = kf_ref[pl.ds(r0, S5_GROUP), :]
        kb = kb_ref[pl.ds(r0, S5_GROUP), :]
        row0 = pl.multiple_of(i * t, t)
        for o in range(S5_GROUP):
            fwd = pltpu.roll(jnp.broadcast_to(kf[o:o + 1, :], (t, t)), 0, 1, stride=1, stride_axis=0)
            bwd = pltpu.roll(jnp.broadcast_to(kb[o:o + 1, :], (t, t)), 0, 1, stride=1, stride_axis=0)
            m_ref[pl.ds(row0, t), o * t:(o + 1) * t] = jnp.where(causal, fwd, bwd).astype(BF16)
        return carry

    lax.fori_loop(0, S5_GROUP, fill, 0)
    y = jnp.dot(_s5_chunk_rows(u_ref), m_ref[...], preferred_element_type=F32)
    hin = jnp.concatenate([hre_ref[...], him_ref[...]], axis=1).astype(BF16)
    y = y + jnp.dot(hin, e_ref[...], preferred_element_type=F32)
    for o in range(S5_GROUP):
        y_ref[o] = y[:, o * t:(o + 1) * t]


def _s5(u_t, tables, *, n_batch, seq, ctx_len):
    kf, kb, f, e, dec_re, dec_im = tables
    g = S5_GROUPS
    nc = u_t.shape[1]
    tw = S5_GROUP * S5_T
    sw = 2 * S5_STATE
    gspec = lambda shape: pl.BlockSpec((None,) + shape, lambda i: (i, 0, 0))
    uspec = pl.BlockSpec((S5_GROUP, nc, S5_T), lambda i: (i, 0, 0))
    s_re, s_im = pl.pallas_call(
        _s5_state_kernel,
        grid=(g,),
        in_specs=[uspec, gspec((tw, 2 * sw))],
        out_specs=[pl.BlockSpec((nc, sw), lambda i: (0, i)), pl.BlockSpec((nc, sw), lambda i: (0, i))],
        out_shape=[jax.ShapeDtypeStruct((nc, g * sw), F32)] * 2,
        compiler_params=_params("parallel"),
        name="s5_chunk_state",
    )(u_t, f)
    gb = SUBLANES
    blk3 = pl.BlockSpec((nc, gb, sw), lambda i: (0, i, 0))
    dspec = pl.BlockSpec((gb, sw), lambda i: (i, 0))
    h_re, h_im = pl.pallas_call(
        functools.partial(_s5_rec_kernel, n_batch=n_batch, nlat=seq // S5_T, nctx=ctx_len // S5_T),
        grid=(g // gb,),
        in_specs=[blk3, blk3, dspec, dspec],
        out_specs=[blk3, blk3],
        out_shape=[jax.ShapeDtypeStruct((nc, g, sw), F32)] * 2,
        compiler_params=_params("parallel"),
        name="s5_chunk_scan",
    )(s_re.reshape(nc, g, sw), s_im.reshape(nc, g, sw), dec_re, dec_im)
    hspec = pl.BlockSpec((nc, sw), lambda i: (0, i))
    tap = gspec((S5_GROUP * S5_GROUP, S5_T))
    return pl.pallas_call(
        _s5_out_kernel,
        grid=(g,),
        in_specs=[uspec, hspec, hspec, tap, tap, gspec((2 * sw, tw))],
        out_specs=uspec,
        out_shape=jax.ShapeDtypeStruct(u_t.shape, F32),
        scratch_shapes=[pltpu.VMEM((tw, tw), BF16)],
        compiler_params=_params("parallel"),
        name="s5_out",
    )(u_t, h_re.reshape(nc, g * sw), h_im.reshape(nc, g * sw), kf, kb, e)


def _ssd_kernel(*refs, rev, finish, n_lat_chunks, n_ctx_chunks):
    if finish:
        (xo_ref, xp_ref, xn_ref, dt_ref, z_ref, yb_ref, cw_ref, cb_ref, bias_ref, a_ref, ex_ref, d_ref, ng_ref,
         o_ref, xe_ref, h_ref, yd_ref) = refs
    else:
        (xo_ref, xp_ref, xn_ref, dt_ref, cw_ref, cb_ref, bias_ref, a_ref, ex_ref,
         o_ref, xe_ref, h_ref, yd_ref) = refs
    q = SSD_Q
    halo = SUBLANES
    i = pl.program_id(1)

    @pl.when(i == 0)
    def _():
        h_ref[...] = jnp.zeros_like(h_ref)

    in_ctx = i < n_ctx_chunks
    pos = jnp.where(in_ctx, i, i - n_ctx_chunks)
    length = jnp.where(in_ctx, n_ctx_chunks, n_lat_chunks)
    if rev:
        pos = length - 1 - pos
    has_prev = pos > 0
    has_next = pos < length - 1

    xe_ref[0:halo, :] = xp_ref[...] * has_prev.astype(F32)
    xe_ref[halo:halo + q, :] = xo_ref[...]
    xe_ref[halo + q:, :] = xn_ref[...] * has_next.astype(F32)
    xe = xe_ref[...]
    mid = SSD_CONV // 2
    acc = cb_ref[...] + cw_ref[mid:mid + 1, :] * xo_ref[...]
    for k in range(SSD_CONV):
        if k != mid:
            acc = acc + cw_ref[k:k + 1, :] * pltpu.roll(xe, (mid - k) % (q + 2 * halo), 0)[halo:halo + q]
    xbc = _silu(acc)
    xs = xbc[:, :SSD_DIM]
    bm_f = xbc[:, SSD_DIM:SSD_DIM + SSD_BC]
    bm = bm_f.astype(BF16)
    cm = xbc[:, SSD_DIM + SSD_BC:].astype(BF16)

    raw = dt_ref[...] + bias_ref[...]
    dt = jnp.maximum(raw, 0.0) + jnp.log1p(jnp.exp(-jnp.abs(raw)))
    ad = dt * a_ref[...]
    li = lax.broadcasted_iota(jnp.int32, (q, q), 0)
    si = lax.broadcasted_iota(jnp.int32, (q, q), 1)
    keep = (si >= li) if rev else (si <= li)
    tri = keep.astype(F32)
    a_cs = jnp.dot(tri, ad, precision=HIGHEST, preferred_element_type=F32)
    a_cs_t = a_cs.T
    dt_t = dt.T
    edge = 0 if rev else q - 1
    tot = a_cs[edge:edge + 1, :]
    scal = jnp.concatenate([dt * jnp.exp(tot - a_cs), jnp.exp(a_cs)], axis=0)
    s_hi, s_lo = _split_bf16(scal)
    ex = ex_ref[...]
    scal_x = (jnp.dot(s_hi, ex, preferred_element_type=F32) + jnp.dot(s_lo, ex, preferred_element_type=F32))
    ds_x = scal_x[:q]
    do_x = scal_x[q:]
    tot_x = do_x[edge:edge + 1, :]

    xs_b = xs.astype(BF16)
    xds_b = (xs * ds_x).astype(BF16)
    rep = SSD_HEADS // SSD_GROUPS
    gw = rep * SSD_HEAD_DIM
    lane0 = SSD_HEADS if rev else 0
    nt = (((1,), (1,)), ((), ()))
    y_parts = []
    for g in range(SSD_GROUPS):
        bg = bm[:, g * SSD_STATE:(g + 1) * SSD_STATE]
        cg = cm[:, g * SSD_STATE:(g + 1) * SSD_STATE]
        cb = lax.dot_general(cg, bg, nt, preferred_element_type=F32)
        for r in range(rep):
            hd = g * rep + r
            col = a_cs[:, lane0 + hd:lane0 + hd + 1]
            row = a_cs_t[lane0 + hd:lane0 + hd + 1, :]
            dec = jnp.exp(jnp.where(keep, col - row, -jnp.inf)) * dt_t[lane0 + hd:lane0 + hd + 1, :]
            yd_ref[:, hd * SSD_HEAD_DIM:(hd + 1) * SSD_HEAD_DIM] = jnp.dot(
                (cb * dec).astype(BF16), xs_b[:, hd * SSD_HEAD_DIM:(hd + 1) * SSD_HEAD_DIM],
                preferred_element_type=F32)
        gs = slice(g * gw, (g + 1) * gw)
        h_g = h_ref[:, gs]
        y_off = jnp.dot(cg, h_g.astype(BF16), preferred_element_type=F32) * do_x[:, gs]
        bg_t = bm_f[:, g * SSD_STATE:(g + 1) * SSD_STATE].T.astype(BF16)
        st = jnp.dot(bg_t, xds_b[:, gs], preferred_element_type=F32)
        h_ref[:, gs] = h_g * tot_x[:, gs] + st
        y_parts.append(y_off)
    y = yd_ref[...] + jnp.concatenate(y_parts, axis=1)

    if finish:
        y = y + yb_ref[...] + d_ref[...] * xs
        y = y * _silu(z_ref[...])
        nw = SSD_DIM // SSD_GROUPS
        outs = []
        for g in range(SSD_GROUPS):
            yg = y[:, g * nw:(g + 1) * nw]
            outs.append(yg * lax.rsqrt(jnp.mean(yg * yg, axis=-1, keepdims=True) + NORM_EPS))
        o_ref[...] = (jnp.concatenate(outs, axis=1) * ng_ref[...]).astype(BF16)
    else:
        o_ref[...] = y


def _ssd_pass(p, y_other, vecs, *, rev, n_batch, seq, ctx_len, col_dt):
    n = p.shape[0]
    q = SSD_Q
    halo = SUBLANES
    nl, ncx = seq // q, ctx_len // q
    hpq = q // halo
    finish = y_other is not None
    conv_w, conv_b, bias, a_vec, expand, d_vec, norm_g = vecs

    def chunk(b, i):
        pos_c = (ncx - 1 - i) if rev else i
        pos_l = (nl - 1 - (i - ncx)) if rev else (i - ncx)
        return jnp.where(i < ncx, n_batch * nl + b * ncx + pos_c, b * nl + pos_l)

    last_halo = n // halo - 1
    own = lambda col_blk, w: pl.BlockSpec((q, w), lambda b, i: (chunk(b, i), col_blk))
    prev = pl.BlockSpec((halo, SSD_XBC), lambda b, i: (jnp.maximum(chunk(b, i) * hpq - 1, 0), COL_XBC // SSD_XBC))
    nxt = pl.BlockSpec((halo, SSD_XBC),
                       lambda b, i: (jnp.minimum((chunk(b, i) + 1) * hpq, last_halo), COL_XBC // SSD_XBC))
    const = lambda a: pl.BlockSpec(a.shape, lambda b, i: (0, 0))
    in_specs = [own(COL_XBC // SSD_XBC, SSD_XBC), prev, nxt, own(col_dt // LANES, LANES)]
    args = [p, p, p, p]
    if finish:
        in_specs += [own(COL_Z // SSD_DIM, SSD_DIM), own(0, SSD_DIM)]
        args += [p, y_other]
    tail = [conv_w, conv_b, bias, a_vec, expand] + ([d_vec, norm_g] if finish else [])
    in_specs += [const(a) for a in tail]
    args += tail
    return pl.pallas_call(
        functools.partial(_ssd_kernel, rev=rev, finish=finish, n_lat_chunks=nl, n_ctx_chunks=ncx),
        grid=(n_batch, nl + ncx),
        in_specs=in_specs,
        out_specs=own(0, SSD_DIM),
        out_shape=jax.ShapeDtypeStruct((n, SSD_DIM), BF16 if finish else F32),
        scratch_shapes=[pltpu.VMEM((q + 2 * halo, SSD_XBC), F32),
                        pltpu.VMEM((SSD_STATE, SSD_DIM), F32),
                        pltpu.VMEM((q, SSD_DIM), F32)],
        compiler_params=_params("parallel", "arbitrary"),
        name="ssd_bwd" if rev else "ssd_fwd",
    )(*args)


def _ssd(p, conv_w, conv_b, dt_bias, a_log, d_skip, norm_g, *, n_batch, seq, ctx_len, col_dt):
    cw = jnp.zeros((SUBLANES, SSD_XBC), F32).at[:SSD_CONV].set(conv_w.astype(F32))
    cb = conv_b.astype(F32).reshape(1, SSD_XBC)
    bias = jnp.zeros((1, LANES), F32).at[0, :2 * SSD_HEADS].set(dt_bias.astype(F32).reshape(-1))
    a = -jnp.exp(a_log.astype(F32))
    d_vec = jnp.repeat(d_skip.astype(F32), SSD_HEAD_DIM).reshape(1, SSD_DIM)
    ng = norm_g.astype(F32).reshape(1, SSD_DIM)
    head_of_channel = jnp.arange(SSD_DIM) // SSD_HEAD_DIM
    outs = {}
    for rev in (True, False):
        lane0 = SSD_HEADS if rev else 0
        a_vec = jnp.zeros((1, LANES), F32).at[0, lane0:lane0 + SSD_HEADS].set(a[1 if rev else 0])
        expand = (jnp.arange(LANES)[:, None] == (head_of_channel[None, :] + lane0)).astype(BF16)
        outs[rev] = _ssd_pass(p, None if rev else outs[True], (cw, cb, bias, a_vec, expand, d_vec, ng),
                              rev=rev, n_batch=n_batch, seq=seq, ctx_len=ctx_len, col_dt=col_dt)
    return outs[False]


def _s5_glu_kernel(y_ref, u_ref, d_ref, w_ref, o_ref):
    y = y_ref[...] + d_ref[...] * u_ref[...]
    g = 0.5 * y * (1.0 + jnp.tanh(np.sqrt(2.0 / np.pi).astype(np.float32) * (y + 0.044715 * (y * y * y))))
    gate = _sigmoid(jnp.dot(w_ref[...], g.astype(BF16), preferred_element_type=F32))
    o_ref[...] = (g * gate).T.astype(BF16)


def _s5_glu(y_t, u_t, d_s5, w_glu_t, *, n_rows):
    tm = ROW_TILE
    cm = pl.BlockSpec((S5_DIM, tm), lambda i: (0, i))
    return pl.pallas_call(
        _s5_glu_kernel,
        grid=(n_rows // tm,),
        in_specs=[cm, cm,
                  pl.BlockSpec((S5_DIM, 1), lambda i: (0, 0)),
                  pl.BlockSpec((S5_DIM, S5_DIM), lambda i: (0, 0))],
        out_specs=pl.BlockSpec((tm, S5_DIM), lambda i: (i, 0)),
        out_shape=jax.ShapeDtypeStruct((n_rows, S5_DIM), BF16),
        compiler_params=_params("parallel"),
        name="s5_glu",
    )(y_t, u_t, d_s5, w_glu_t)


def _merge_kernel(h_ref, mod_ref, oa_ref, os_ref, od_ref, g0_ref, g1_ref, g2_ref,
                  wa_ref, ws_ref, wd_ref, wo_ref, o_ref, acc_ref):
    j = pl.program_id(1)

    @pl.when(j == 0)
    def _():
        acc_ref[...] = jnp.zeros_like(acc_ref)

    m = (_sigmoid(g0_ref[...]) * jnp.dot(oa_ref[...], wa_ref[...], preferred_element_type=F32)
         + _sigmoid(g1_ref[...]) * jnp.dot(os_ref[...], ws_ref[...], preferred_element_type=F32)
         + _sigmoid(g2_ref[...]) * jnp.dot(od_ref[...], wd_ref[...], preferred_element_type=F32))
    acc_ref[...] += jnp.dot(m.astype(BF16), wo_ref[...], preferred_element_type=F32)

    @pl.when(j == pl.num_programs(1) - 1)
    def _():
        o_ref[...] = h_ref[...] + mod_ref[5:6, :] * acc_ref[...]


def _merge(h, mod, p, oa, os5, ossd, w_a, w_s, w_d, w_o, *, n_rows, tiles_per_batch, n_batch):
    d = h.shape[1]
    tm, tc = ROW_TILE, 512
    nj = d // tc
    midx = lambda i, j: (jnp.minimum(i // tiles_per_batch, n_batch), 0, 0)
    row = lambda w: pl.BlockSpec((tm, w), lambda i, j: (i, 0))
    gate = lambda br: pl.BlockSpec((tm, tc), lambda i, j: (i, (COL_GATE + br * d) // tc + j))
    wcol = lambda k: pl.BlockSpec((k, tc), lambda i, j: (0, j))
    return pl.pallas_call(
        _merge_kernel,
        grid=(n_rows // tm, nj),
        in_specs=[row(d), pl.BlockSpec((None, N_MOD, d), midx),
                  row(Q_DIM), row(S5_DIM), row(SSD_DIM), gate(0), gate(1), gate(2),
                  wcol(Q_DIM), wcol(S5_DIM), wcol(SSD_DIM),
                  pl.BlockSpec((tc, d), lambda i, j: (j, 0))],
        out_specs=row(d),
        out_shape=jax.ShapeDtypeStruct((n_rows, d), F32),
        scratch_shapes=[pltpu.VMEM((tm, d), F32)],
        compiler_params=_params("parallel", "arbitrary"),
        name="merge",
    )(h, mod, oa, os5, ossd, p, p, p, w_a, w_s, w_d, w_o)


def _rope_tables(seq, extra_rows):
    rows = seq // GRID_W
    row = jnp.repeat(jnp.arange(rows, dtype=F32), GRID_W)
    col = jnp.tile(jnp.arange(GRID_W, dtype=F32), rows)
    axis_dim = HEAD_DIM // 2
    inv_freq = ROPE_BASE ** (-jnp.arange(0, axis_dim, 2, dtype=F32) / axis_dim)
    ang_r = row[:, None] * inv_freq[None, :]
    ang_c = col[:, None] * inv_freq[None, :]
    cos = jnp.concatenate([jnp.cos(ang_r), jnp.cos(ang_c)] * 2, axis=1)
    sin = jnp.concatenate([-jnp.sin(ang_r), -jnp.sin(ang_c), jnp.sin(ang_r), jnp.sin(ang_c)], axis=1)
    cos = jnp.concatenate([cos, jnp.ones((extra_rows, HEAD_DIM), F32)], axis=0)
    sin = jnp.concatenate([sin, jnp.zeros((extra_rows, HEAD_DIM), F32)], axis=0)
    return cos, sin


def _pair_major(t):
    lead = t.shape[:-1]
    t = t.reshape(lead + (-1, 2, 2, HEAD_DIM // 4))
    return jnp.swapaxes(t, -3, -2).reshape(lead + (-1,))


def _permute_w_in(w, d):
    v0 = Q_DIM + KV_DIM
    s0 = Q_DIM + 2 * KV_DIM
    z0 = s0 + S5_DIM
    x0 = z0 + SSD_DIM
    t0 = x0 + SSD_XBC
    g0 = t0 + 2 * SSD_HEADS
    g1 = g0 + N_BRANCH * d
    parts = [_pair_major(w[:, :v0]), w[:, v0:s0], w[:, x0:t0], w[:, t0:g0],
             jnp.zeros((w.shape[0], DT_SLOT - 2 * SSD_HEADS), w.dtype), w[:, z0:x0], w[:, g0:g1]]
    return jnp.concatenate(parts, axis=1).astype(BF16), w[:, s0:z0].T.astype(BF16)


def kernel(x, c, ctx, c_ctx, w_ada, b_ada, norm_g, w_ffn_gate, w_ffn_up, w_ffn_down, w_in, qk_norm_g, attn_sink,
           w_attn_o, s5_a_re, s5_a_im, s5_log_step, s5_b_re, s5_b_im, s5_c_re, s5_c_im, s5_d, s5_w_glu, w_s5_o,
           ssd_conv_w, ssd_conv_b, ssd_dt_bias, ssd_a_log, ssd_d, ssd_norm_g, w_ssd_o, w_out):
    n_batch, seq, d = x.shape
    ctx_len = ctx.shape[1]
    depth = w_ada.shape[0]
    n_lat, n_ctx = n_batch * seq, n_batch * ctx_len
    n = n_lat + n_ctx
    assert seq % ROW_TILE == 0 and n_ctx % ROW_TILE == 0 and ctx_len % SSD_Q == 0
    assert (COL_GATE + N_BRANCH * d) % PROJ_TN == 0 and n_batch + 1 <= SUBLANES

    h = jnp.concatenate([x.reshape(n_lat, d), ctx.reshape(n_ctx, d)], axis=0)
    c_rows = jnp.zeros((SUBLANES, d), F32).at[:n_batch].set(c.astype(F32)).at[n_batch].set(c_ctx.astype(F32))
    mods = _ada(c_rows, w_ada, b_ada)[:, :n_batch + 1].reshape(depth, n_batch + 1, N_MOD, d)
    cos_t, sin_t = _rope_tables(seq, ROW_TILE)

    tpb = seq // ROW_TILE
    for l in range(depth):
        last = l == depth - 1
        ffn_w = lambda k: (w_ffn_gate[l, k].astype(BF16), w_ffn_up[l, k].astype(BF16), w_ffn_down[l, k].astype(BF16))
        g = norm_g[l].astype(F32)
        h = _ffn(h, mods[l], g, *ffn_w(0), rows=(0, 1, 2, 0), n_rows=n, tiles_per_batch=tpb, n_batch=n_batch)
        p, u_t = _proj(h, mods[l], g, *_permute_w_in(w_in[l], d), tiles_per_batch=tpb, n_batch=n_batch)

        qkv = _prep(p, cos_t, sin_t, _pair_major(qk_norm_g[l].astype(F32)), tiles_lat=n_lat // ROW_TILE,
                    tiles_per_batch=tpb)
        oa = _attention(qkv, attn_sink[l].astype(F32), n_batch=n_batch, seq=seq, ctx_len=ctx_len, with_ctx=not last)

        tables = _s5_tables(s5_a_re[l], s5_a_im[l], s5_log_step[l], s5_b_re[l], s5_b_im[l], s5_c_re[l], s5_c_im[l])
        y_t = _s5(u_t.reshape(S5_DIM, n // S5_T, S5_T), tables, n_batch=n_batch, seq=seq, ctx_len=ctx_len)

        ossd = _ssd(p, ssd_conv_w[l], ssd_conv_b[l], ssd_dt_bias[l], ssd_a_log[l], ssd_d[l], ssd_norm_g[l],
                    n_batch=n_batch, seq=seq, ctx_len=ctx_len, col_dt=COL_DT)

        n_out = n_lat if last else n
        os5 = _s5_glu(y_t.reshape(S5_DIM, n), u_t, s5_d[l].astype(F32).reshape(S5_DIM, 1),
                      s5_w_glu[l].T.astype(BF16), n_rows=n_out)
        h = _merge(h, mods[l], p, oa, os5, ossd,
                   w_attn_o[l].astype(BF16), w_s5_o[l].astype(BF16), w_ssd_o[l].astype(BF16), w_out[l].astype(BF16),
                   n_rows=n_out, tiles_per_batch=tpb, n_batch=n_batch)
        h = _ffn(h, mods[l], g, *ffn_w(1), rows=(6, 7, 8, 2), n_rows=n_out, tiles_per_batch=tpb, n_batch=n_batch)
    return h.reshape(n_batch, seq, d)
```

```python
import functools

import numpy as np
import jax
import jax.numpy as jnp
from jax import lax
from jax.experimental import pallas as pl
from jax.experimental.pallas import tpu as pltpu

F32 = jnp.float32
BF16 = jnp.bfloat16
HIGHEST = lax.Precision.HIGHEST

GRID_W = 64
N_MOD = 9
NORM_EPS = 1e-6
ATTN_HEADS = 12
ATTN_KV_HEADS = 4
HEAD_DIM = 128
ATTN_BLOCK = 128
ROPE_BASE = 10000.0
S5_DIM = 1024
S5_GROUP = 16
S5_GROUPS = S5_DIM // S5_GROUP
S5_STATE = 64
SSD_DIM = 1536
SSD_HEAD_DIM = 64
SSD_HEADS = SSD_DIM // SSD_HEAD_DIM
SSD_GROUPS = 4
SSD_STATE = 128
SSD_CONV = 5
N_BRANCH = 3
Q_DIM = ATTN_HEADS * HEAD_DIM
KV_DIM = ATTN_KV_HEADS * HEAD_DIM
SSD_BC = SSD_GROUPS * SSD_STATE
SSD_XBC = SSD_DIM + 2 * SSD_BC

VMEM_LIMIT_BYTES = 56 * 1024 * 1024
LANES = 128
SUBLANES = 8

COL_QKV = 0
COL_XBC = Q_DIM + 2 * KV_DIM
COL_DT = COL_XBC + SSD_XBC
DT_SLOT = 1024
COL_Z = COL_DT + DT_SLOT
COL_GATE = COL_Z + SSD_DIM
PROJ_TN = 1536

ROW_TILE = 512
S5_T = 128
SSD_Q = 128


def _params(*sem):
    return pltpu.CompilerParams(dimension_semantics=sem, vmem_limit_bytes=VMEM_LIMIT_BYTES)


def _sigmoid(v):
    return 0.5 * (1.0 + jnp.tanh(0.5 * v))


def _silu(v):
    return v * _sigmoid(v)


def _split_bf16(v):
    hi = v.astype(BF16)
    return hi, (v - hi.astype(F32)).astype(BF16)


def _rms_mod(x, g, shift, scale):
    y = x * lax.rsqrt(jnp.mean(x * x, axis=-1, keepdims=True) + NORM_EPS) * g
    return y * (1.0 + scale) + shift


def _ada_kernel(c_ref, w_ref, b_ref, o_ref):
    s = _silu(c_ref[...]).astype(BF16)
    o_ref[...] = jnp.dot(s, w_ref[...].astype(BF16), preferred_element_type=F32) + b_ref[...]


def _ada(c_rows, w_ada, b_ada):
    depth, d, n = w_ada.shape
    tn = 1024
    return pl.pallas_call(
        _ada_kernel,
        grid=(depth, n // tn),
        in_specs=[pl.BlockSpec((SUBLANES, d), lambda l, j: (0, 0)),
                  pl.BlockSpec((None, d, tn), lambda l, j: (l, 0, j)),
                  pl.BlockSpec((None, 1, tn), lambda l, j: (l, 0, j))],
        out_specs=pl.BlockSpec((None, SUBLANES, tn), lambda l, j: (l, 0, j)),
        out_shape=jax.ShapeDtypeStruct((depth, SUBLANES, n), F32),
        compiler_params=_params("parallel", "parallel"),
        name="ada_mod",
    )(c_rows, w_ada, b_ada.reshape(depth, 1, n))


def _ffn_kernel(*refs, rows, tiles_lat):
    r_shift, r_scale, r_gate, r_norm = rows
    j = pl.program_id(1)
    if tiles_lat is None:
        h_ref, mod_ref, g_ref, wg_ref, wu_ref, wd_ref, o_ref, xn_ref, acc_ref = refs
    else:
        lat_ref, ctx_ref, mod_ref, g_ref, wg_ref, wu_ref, wd_ref, o_ref, xn_ref, acc_ref, h_ref = refs
        is_lat = pl.program_id(0) < tiles_lat

        @pl.when(jnp.logical_and(j == 0, is_lat))
        def _():
            h_ref[...] = lat_ref[...]

        @pl.when(jnp.logical_and(j == 0, jnp.logical_not(is_lat)))
        def _():
            h_ref[...] = ctx_ref[...]

    @pl.when(j == 0)
    def _():
        xn = _rms_mod(h_ref[...], g_ref[r_norm:r_norm + 1, :], mod_ref[r_shift:r_shift + 1, :],
                      mod_ref[r_scale:r_scale + 1, :])
        xn_ref[...] = xn.astype(BF16)
        acc_ref[...] = jnp.zeros_like(acc_ref)

    xn = xn_ref[...]
    a = jnp.dot(xn, wg_ref[...], preferred_element_type=F32)
    b = jnp.dot(xn, wu_ref[...], preferred_element_type=F32)
    t = (_silu(a) * b).astype(BF16)
    acc_ref[...] += jnp.dot(t, wd_ref[...], preferred_element_type=F32)

    @pl.when(j == pl.num_programs(1) - 1)
    def _():
        o_ref[...] = h_ref[...] + 0.5 * mod_ref[r_gate:r_gate + 1, :] * acc_ref[...]


def _ffn(h, mod, g, wg, wu, wd, *, rows, n_rows, tiles_per_batch, n_batch, h_ctx=None):
    d = h.shape[1]
    f = wg.shape[1]
    tm, tf = ROW_TILE, 512
    midx = lambda i, j: (jnp.minimum(i // tiles_per_batch, n_batch), 0, 0)
    scratch = [pltpu.VMEM((tm, d), BF16), pltpu.VMEM((tm, d), F32)]
    if h_ctx is None:
        tiles_lat = None
        srcs, src_specs = [h], [pl.BlockSpec((tm, d), lambda i, j: (i, 0))]
    else:
        tiles_lat = h.shape[0] // tm
        srcs = [h, h_ctx]
        src_specs = [pl.BlockSpec((tm, d), lambda i, j: (jnp.minimum(i, tiles_lat - 1), 0)),
                     pl.BlockSpec((tm, d), lambda i, j: (jnp.maximum(i - tiles_lat, 0), 0))]
        scratch.append(pltpu.VMEM((tm, d), F32))
    return pl.pallas_call(
        functools.partial(_ffn_kernel, rows=rows, tiles_lat=tiles_lat),
        grid=(n_rows // tm, f // tf),
        in_specs=src_specs + [
                  pl.BlockSpec((None, N_MOD, d), midx),
                  pl.BlockSpec(g.shape, lambda i, j: (0, 0)),
                  pl.BlockSpec((d, tf), lambda i, j: (0, j)),
                  pl.BlockSpec((d, tf), lambda i, j: (0, j)),
                  pl.BlockSpec((tf, d), lambda i, j: (j, 0))],
        out_specs=pl.BlockSpec((tm, d), lambda i, j: (i, 0)),
        out_shape=jax.ShapeDtypeStruct((n_rows, d), F32),
        scratch_shapes=scratch,
        compiler_params=_params("parallel", "arbitrary"),
        name="ffn",
    )(*srcs, mod, g, wg, wu, wd)


def _proj_kernel(h_ref, mod_ref, g_ref, w_ref, ws_ref, o_ref, ut_ref, u_ref):
    @pl.when(pl.program_id(1) == 0)
    def _():
        u = _rms_mod(h_ref[...], g_ref[1:2, :], mod_ref[3:4, :], mod_ref[4:5, :]).astype(BF16)
        u_ref[...] = u
        ut_ref[...] = lax.dot_general(ws_ref[...], u, (((1,), (1,)), ((), ())), preferred_element_type=F32)

    o_ref[...] = jnp.dot(u_ref[...], w_ref[...], preferred_element_type=F32)


def _proj(h, mod, g, w, ws_t, *, tiles_per_batch, n_batch):
    n, d = h.shape
    ncol = w.shape[1]
    tm, tn = ROW_TILE, PROJ_TN
    midx = lambda i, j: (jnp.minimum(i // tiles_per_batch, n_batch), 0, 0)
    return pl.pallas_call(
        _proj_kernel,
        grid=(n // tm, ncol // tn),
        in_specs=[pl.BlockSpec((tm, d), lambda i, j: (i, 0)),
                  pl.BlockSpec((None, N_MOD, d), midx),
                  pl.BlockSpec(g.shape, lambda i, j: (0, 0)),
                  pl.BlockSpec((d, tn), lambda i, j: (0, j)),
                  pl.BlockSpec(ws_t.shape, lambda i, j: (0, 0))],
        out_specs=[pl.BlockSpec((tm, tn), lambda i, j: (i, j)),
                   pl.BlockSpec((S5_DIM, tm), lambda i, j: (0, i))],
        out_shape=[jax.ShapeDtypeStruct((n, ncol), F32), jax.ShapeDtypeStruct((S5_DIM, n), F32)],
        scratch_shapes=[pltpu.VMEM((tm, d), BF16)],
        compiler_params=_params("parallel", "arbitrary"),
        name="in_proj",
    )(h, mod, g, w, ws_t)


def _prep_kernel(p_ref, cos_ref, sin_ref, g_ref, o_ref):
    cos = cos_ref[...]
    sin = sin_ref[...]
    scale = HEAD_DIM ** -0.5
    mean_mat = jnp.full((HEAD_DIM, HEAD_DIM), 1.0 / HEAD_DIM, BF16)
    for hd in range(ATTN_HEADS + ATTN_KV_HEADS):
        sl = slice(hd * HEAD_DIM, (hd + 1) * HEAD_DIM)
        x = p_ref[:, sl]
        g = g_ref[0:1, :] if hd < ATTN_HEADS else g_ref[1:2, :]
        sq_hi, sq_lo = _split_bf16(x * x)
        ms = (jnp.dot(sq_hi, mean_mat, preferred_element_type=F32)
              + jnp.dot(sq_lo, mean_mat, preferred_element_type=F32))
        y = x * lax.rsqrt(ms + NORM_EPS) * g
        y = y * cos + pltpu.roll(y, HEAD_DIM // 2, 1) * sin
        if hd < ATTN_HEADS:
            y = y * scale
        o_ref[:, sl] = y.astype(BF16)
    vs = slice(Q_DIM + KV_DIM, Q_DIM + 2 * KV_DIM)
    o_ref[:, vs] = p_ref[:, vs].astype(BF16)


def _prep(p, cos_t, sin_t, qk_g, *, tiles_lat, tiles_per_batch):
    n = p.shape[0]
    tm = ROW_TILE
    w = Q_DIM + 2 * KV_DIM
    tidx = lambda i: (jnp.where(i < tiles_lat, i % tiles_per_batch, tiles_per_batch), 0)
    return pl.pallas_call(
        _prep_kernel,
        grid=(n // tm,),
        in_specs=[pl.BlockSpec((tm, w), lambda i: (i, 0)),
                  pl.BlockSpec((tm, HEAD_DIM), tidx),
                  pl.BlockSpec((tm, HEAD_DIM), tidx),
                  pl.BlockSpec(qk_g.shape, lambda i: (0, 0))],
        out_specs=pl.BlockSpec((tm, w), lambda i: (i, 0)),
        out_shape=jax.ShapeDtypeStruct((n, w), BF16),
        compiler_params=_params("parallel"),
        name="qk_prep",
    )(p, cos_t, sin_t, qk_g)


def _attn_kernel(sink_ref, q_ref, kp_ref, ko_ref, kn_ref, vp_ref, vo_ref, vn_ref, kc_ref, vc_ref, o_ref, *, nb):
    n = pl.program_id(1)
    blk = ATTN_BLOCK
    grp = ATTN_HEADS // ATTN_KV_HEADS
    is_lat = n < nb
    has_prev = jnp.logical_and(is_lat, n > 0)
    has_next = jnp.logical_and(is_lat, n < nb - 1)
    rows = grp * blk
    r = lax.broadcasted_iota(jnp.int32, (rows, 3 * blk), 0)
    kk = lax.broadcasted_iota(jnp.int32, (rows, 3 * blk), 1)
    i = r & (blk - 1)
    far = 4 * blk
    off_prev = jnp.where(has_prev, 0, far)
    off_own = jnp.where(is_lat, 0, far)
    off_next = jnp.where(has_next, 0, far)
    mask = (((kk < blk) & (kk >= i + off_prev))
            | ((kk >= blk + off_own) & (kk < 2 * blk))
            | ((kk >= 2 * blk) & (kk - 2 * blk <= i - off_next)))
    rr = lax.broadcasted_iota(jnp.int32, (rows, 1), 0)
    nt = (((1,), (1,)), ((), ()))
    for kv in range(ATTN_KV_HEADS):
        ks = slice(kv * HEAD_DIM, (kv + 1) * HEAD_DIM)
        qh = jnp.concatenate([q_ref[:, (kv * grp + g) * HEAD_DIM:(kv * grp + g + 1) * HEAD_DIM] for g in range(grp)],
                             axis=0)
        kl = jnp.concatenate([kp_ref[:, ks], ko_ref[:, ks], kn_ref[:, ks]], axis=0)
        vl = jnp.concatenate([vp_ref[:, ks], vo_ref[:, ks], vn_ref[:, ks]], axis=0)
        s_loc = lax.dot_general(qh, kl, nt, preferred_element_type=F32)
        s_ctx = lax.dot_general(qh, kc_ref[:, ks], nt, preferred_element_type=F32)
        s_loc = jnp.where(mask, s_loc, -jnp.inf)
        sink = jnp.full((rows, 1), sink_ref[kv * grp], F32)
        for g in range(1, grp):
            sink = jnp.where(rr >= g * blk, sink_ref[kv * grp + g], sink)
        m = jnp.maximum(jnp.maximum(jnp.max(s_loc, axis=-1, keepdims=True),
                                    jnp.max(s_ctx, axis=-1, keepdims=True)), sink)
        p_loc = jnp.exp(s_loc - m)
        p_ctx = jnp.exp(s_ctx - m)
        denom = (jnp.sum(p_loc, axis=-1, keepdims=True) + jnp.sum(p_ctx, axis=-1, keepdims=True)
                 + jnp.exp(sink - m))
        o = (jnp.dot(p_loc.astype(BF16), vl, preferred_element_type=F32)
             + jnp.dot(p_ctx.astype(BF16), vc_ref[:, ks], preferred_element_type=F32))
        o = o * (1.0 / denom)
        for g in range(grp):
            hs = slice((kv * grp + g) * HEAD_DIM, (kv * grp + g + 1) * HEAD_DIM)
            o_ref[:, hs] = o[g * blk:(g + 1) * blk, :].astype(BF16)


def _attention(qkv, sink, *, n_batch, seq, ctx_len, with_ctx):
    n = qkv.shape[0]
    blk = ATTN_BLOCK
    nb = seq // blk
    ncb = ctx_len // blk
    steps = nb + (ncb if with_ctx else 0)
    ctx0 = n_batch * nb
    kcol = Q_DIM // KV_DIM

    def own(b, s):
        return jnp.where(s < nb, b * nb + s, ctx0 + b * ncb + (s - nb))

    def prev(b, s):
        return jnp.where(s < nb, b * nb + jnp.maximum(s - 1, 0), ctx0 + b * ncb)

    def nxt(b, s):
        return jnp.where(s < nb, b * nb + jnp.minimum(s + 1, nb - 1), ctx0 + b * ncb)

    ctx_blk = lambda b, s: ((n_batch * seq) // ctx_len + b)
    kv_spec = lambda f, col: pl.BlockSpec((blk, KV_DIM), lambda b, s: (f(b, s), col))
    return pl.pallas_call(
        functools.partial(_attn_kernel, nb=nb),
        grid=(n_batch, steps),
        in_specs=[pl.BlockSpec(memory_space=pltpu.SMEM),
                  pl.BlockSpec((blk, Q_DIM), lambda b, s: (own(b, s), 0)),
                  kv_spec(prev, kcol), kv_spec(own, kcol), kv_spec(nxt, kcol),
                  kv_spec(prev, kcol + 1), kv_spec(own, kcol + 1), kv_spec(nxt, kcol + 1),
                  pl.BlockSpec((ctx_len, KV_DIM), lambda b, s: (ctx_blk(b, s), kcol)),
                  pl.BlockSpec((ctx_len, KV_DIM), lambda b, s: (ctx_blk(b, s), kcol + 1))],
        out_specs=pl.BlockSpec((blk, Q_DIM), lambda b, s: (own(b, s), 0)),
        out_shape=jax.ShapeDtypeStruct((n, Q_DIM), BF16),
        compiler_params=_params("parallel", "parallel"),
        name="window_attn",
    )(sink, qkv, qkv, qkv, qkv, qkv, qkv, qkv, qkv, qkv)


def _s5_tables(a_re, a_im, log_step, b_re, b_im, c_re, c_im):
    t = S5_T
    step = jnp.exp(log_step.astype(F32))[..., None]
    a_re = a_re.astype(F32)
    a_im = a_im.astype(F32)
    k = jnp.arange(t + 1, dtype=F32)[:, None, None, None]
    mag = jnp.exp(k * (a_re * step)[None])
    ang = k * (a_im * step)[None]
    pr, pi = mag * jnp.cos(ang), mag * jnp.sin(ang)
    den = a_re * a_re + a_im * a_im
    nr, ni = pr[1] - 1.0, pi[1]
    wr = (nr * a_re + ni * a_im) / den
    wi = (ni * a_re - nr * a_im) / den
    b_re = b_re.astype(F32)[None]
    b_im = b_im.astype(F32)[None]
    wbr = wr[..., None] * b_re - wi[..., None] * b_im
    wbi = wr[..., None] * b_im + wi[..., None] * b_re
    c_re = c_re.astype(F32)
    c_im = c_im.astype(F32)
    grp = S5_GROUP

    wbr_t = jnp.swapaxes(wbr, 2, 3)[:, :, :, None, :]
    wbi_t = jnp.swapaxes(wbi, 2, 3)[:, :, :, None, :]
    qr = (c_re[None, :, None] * wbr_t - c_im[None, :, None] * wbi_t).reshape(2, S5_GROUPS, grp * grp, S5_STATE)
    qi = (c_re[None, :, None] * wbi_t + c_im[None, :, None] * wbr_t).reshape(2, S5_GROUPS, grp * grp, S5_STATE)

    def taps(d, pw_r, pw_i):
        return (jnp.einsum('gmp,kgp->gmk', qr[d], pw_r, precision=HIGHEST)
                - jnp.einsum('gmp,kgp->gmk', qi[d], pw_i, precision=HIGHEST))

    kf = taps(0, pr[:t, 0], pi[:t, 0])
    kf = kf.at[:, :, 0].add(jnp.sum(qr[1], axis=-1))
    kb = taps(1, pr[t:0:-1, 1], pi[t:0:-1, 1])

    def cmul(xr, xi, yr, yi):
        return xr * yr - xi * yi, xr * yi + xi * yr

    def rows_in_time(pw):
        return jnp.transpose(pw, (1, 0, 2))[:, None]

    def wb_of(d, w):
        return jnp.swapaxes(w[d], 1, 2)[:, :, None, :]

    ffr, ffi = cmul(rows_in_time(pr[t - 1::-1, 0]), rows_in_time(pi[t - 1::-1, 0]), wb_of(0, wbr), wb_of(0, wbi))
    fbr, fbi = cmul(rows_in_time(pr[:t, 1]), rows_in_time(pi[:t, 1]), wb_of(1, wbr), wb_of(1, wbi))
    f = jnp.concatenate([ffr, fbr, ffi, fbi], axis=-1).reshape(S5_GROUPS, grp * t, 4 * S5_STATE)

    def state_rows(pw_r, pw_i):
        pw_r = jnp.transpose(pw_r, (1, 2, 0))[:, :, None, :]
        pw_i = jnp.transpose(pw_i, (1, 2, 0))[:, :, None, :]
        cr = jnp.swapaxes(c_re, 1, 2)[..., None]
        ci = jnp.swapaxes(c_im, 1, 2)[..., None]
        return cr * pw_r - ci * pw_i, -(cr * pw_i + ci * pw_r)

    efr, efi = state_rows(pr[1:t + 1, 0], pi[1:t + 1, 0])
    ebr, ebi = state_rows(pr[t:0:-1, 1], pi[t:0:-1, 1])
    e = jnp.concatenate([efr, ebr, efi, ebi], axis=1).reshape(S5_GROUPS, 4 * S5_STATE, grp * t)
    dec_re = jnp.concatenate([pr[t, 0], pr[t, 1]], axis=-1)
    dec_im = jnp.concatenate([pi[t, 0], pi[t, 1]], axis=-1)
    return kf, kb, f.astype(BF16), e.astype(BF16), dec_re, dec_im


def _s5_chunk_rows(u_ref):
    nc = u_ref.shape[1] // S5_T
    x = pltpu.einshape("icl->cil", u_ref[...].reshape(S5_GROUP, nc, S5_T))
    return x.reshape(nc, S5_GROUP * S5_T).astype(BF16)


def _s5_state_kernel(u_ref, f_ref, sre_ref, sim_ref):
    s = jnp.dot(_s5_chunk_rows(u_ref), f_ref[...], preferred_element_type=F32)
    sre_ref[...] = s[:, :2 * S5_STATE]
    sim_ref[...] = s[:, 2 * S5_STATE:]


def _s5_rec_kernel(sre_ref, sim_ref, are_ref, aim_ref, hre_ref, him_ref, *, n_batch, nlat, nctx):
    are = are_ref[...]
    aim = aim_ref[...]
    fwd = lax.broadcasted_iota(jnp.int32, are.shape, 1) < S5_STATE
    half = S5_STATE

    def body(i, carry):
        out = []
        for b in range(n_batch):
            hr, hi = carry[b]
            lat0 = b * nlat
            ctx0 = n_batch * nlat + b * nctx
            in_ctx = i < nctx
            fi = jnp.where(in_ctx, ctx0 + i, lat0 + i - nctx)
            bi = jnp.where(in_ctx, ctx0 + nctx - 1 - i, lat0 + nlat - 1 - (i - nctx))
            hre_ref[fi, :, 0:half] = hr[:, 0:half]
            hre_ref[bi, :, half:] = hr[:, half:]
            him_ref[fi, :, 0:half] = hi[:, 0:half]
            him_ref[bi, :, half:] = hi[:, half:]
            sr = jnp.where(fwd, sre_ref[fi], sre_ref[bi])
            si = jnp.where(fwd, sim_ref[fi], sim_ref[bi])
            out.append((are * hr - aim * hi + sr, are * hi + aim * hr + si))
        return tuple(out)

    zero = jnp.zeros(are.shape, F32)
    lax.fori_loop(0, nlat + nctx, body, tuple((zero, zero) for _ in range(n_batch)))


def _s5_out_kernel(u_ref, hre_ref, him_ref, kf_ref, kb_ref, e_ref, y_ref, m_ref):
    t = S5_T
    s_idx = lax.broadcasted_iota(jnp.int32, (t, t), 0)
    t_idx = lax.broadcasted_iota(jnp.int32, (t, t), 1)
    causal = t_idx >= s_idx

    def fill(i, carry):
        r0 = pl.multiple_of(i * S5_GROUP, S5_GROUP)
        kf = kf_ref[pl.ds(r0, S5_GROUP), :]
        kb = kb_ref[pl.ds(r0, S5_GROUP), :]
        row0 = pl.multiple_of(i * t, t)
        for o in range(S5_GROUP):
            fwd = pltpu.roll(jnp.broadcast_to(kf[o:o + 1, :], (t, t)), 0, 1, stride=1, stride_axis=0)
            bwd = pltpu.roll(jnp.broadcast_to(kb[o:o + 1, :], (t, t)), 0, 1, stride=1, stride_axis=0)
            m_ref[pl.ds(row0, t), o * t:(o + 1) * t] = jnp.where(causal, fwd, bwd).astype(BF16)
        return carry

    lax.fori_loop(0, S5_GROUP, fill, 0)
    y = jnp.dot(_s5_chunk_rows(u_ref), m_ref[...], preferred_element_type=F32)
    hin = jnp.concatenate([hre_ref[...], him_ref[...]], axis=1).astype(BF16)
    y = y + jnp.dot(hin, e_ref[...], preferred_element_type=F32)
    nc = y.shape[0]
    y = pltpu.einshape("cil->icl", y.reshape(nc, S5_GROUP, t))
    y_ref[...] = y.reshape(S5_GROUP, nc * t)


def _s5(u_t, tables, *, n_batch, seq, ctx_len):
    kf, kb, f, e, dec_re, dec_im = tables
    g = S5_GROUPS
    n = u_t.shape[1]
    nc = n // S5_T
    tw = S5_GROUP * S5_T
    sw = 2 * S5_STATE
    gspec = lambda shape: pl.BlockSpec((None,) + shape, lambda i: (i, 0, 0))
    uspec = pl.BlockSpec((S5_GROUP, n), lambda i: (i, 0))
    s_re, s_im = pl.pallas_call(
        _s5_state_kernel,
        grid=(g,),
        in_specs=[uspec, gspec((tw, 2 * sw))],
        out_specs=[pl.BlockSpec((nc, sw), lambda i: (0, i)), pl.BlockSpec((nc, sw), lambda i: (0, i))],
        out_shape=[jax.ShapeDtypeStruct((nc, g * sw), F32)] * 2,
        compiler_params=_params("parallel"),
        name="s5_chunk_state",
    )(u_t, f)
    gb = SUBLANES
    blk3 = pl.BlockSpec((nc, gb, sw), lambda i: (0, i, 0))
    dspec = pl.BlockSpec((gb, sw), lambda i: (i, 0))
    h_re, h_im = pl.pallas_call(
        functools.partial(_s5_rec_kernel, n_batch=n_batch, nlat=seq // S5_T, nctx=ctx_len // S5_T),
        grid=(g // gb,),
        in_specs=[blk3, blk3, dspec, dspec],
        out_specs=[blk3, blk3],
        out_shape=[jax.ShapeDtypeStruct((nc, g, sw), F32)] * 2,
        compiler_params=_params("parallel"),
        name="s5_chunk_scan",
    )(s_re.reshape(nc, g, sw), s_im.reshape(nc, g, sw), dec_re, dec_im)
    hspec = pl.BlockSpec((nc, sw), lambda i: (0, i))
    tap = gspec((S5_GROUP * S5_GROUP, S5_T))
    return pl.pallas_call(
        _s5_out_kernel,
        grid=(g,),
        in_specs=[uspec, hspec, hspec, tap, tap, gspec((2 * sw, tw))],
        out_specs=uspec,
        out_shape=jax.ShapeDtypeStruct(u_t.shape, F32),
        scratch_shapes=[pltpu.VMEM((tw, tw), BF16)],
        compiler_params=_params("parallel"),
        name="s5_out",
    )(u_t, h_re.reshape(nc, g * sw), h_im.reshape(nc, g * sw), kf, kb, e)


def _ssd_conv_kernel(xo_ref, xp_ref, xn_ref, cw_ref, cb_ref, xs_ref, bc_ref, xe_ref, *, n_batch, n_lat_chunks,
                     n_ctx_chunks):
    q = SSD_Q
    halo = SUBLANES
    c = pl.program_id(0)
    lat_total = n_batch * n_lat_chunks
    is_lat = c < lat_total
    pos = jnp.where(is_lat, lax.rem(c, n_lat_chunks), lax.rem(jnp.maximum(c - lat_total, 0), n_ctx_chunks))
    length = jnp.where(is_lat, n_lat_chunks, n_ctx_chunks)
    has_prev = pos > 0
    has_next = pos < length - 1
    xe_ref[0:halo, :] = xp_ref[...] * has_prev.astype(F32)
    xe_ref[halo:halo + q, :] = xo_ref[...]
    xe_ref[halo + q:, :] = xn_ref[...] * has_next.astype(F32)
    xe = xe_ref[...]
    mid = SSD_CONV // 2
    acc = cb_ref[...] + cw_ref[mid:mid + 1, :] * xo_ref[...]
    for k in range(SSD_CONV):
        if k != mid:
            acc = acc + cw_ref[k:k + 1, :] * pltpu.roll(xe, (mid - k) % (q + 2 * halo), 0)[halo:halo + q]
    xbc = _silu(acc)
    xs_ref[...] = xbc[:, :SSD_DIM]
    bc_ref[...] = xbc[:, SSD_DIM:].astype(BF16)


def _ssd_conv(p, conv_w, conv_b, *, n_batch, seq, ctx_len):
    n = p.shape[0]
    q = SSD_Q
    halo = SUBLANES
    hpq = q // halo
    last_halo = n // halo - 1
    col = COL_XBC // SSD_XBC
    return pl.pallas_call(
        functools.partial(_ssd_conv_kernel, n_batch=n_batch, n_lat_chunks=seq // q, n_ctx_chunks=ctx_len // q),
        grid=(n // q,),
        in_specs=[pl.BlockSpec((q, SSD_XBC), lambda c: (c, col)),
                  pl.BlockSpec((halo, SSD_XBC), lambda c: (jnp.maximum(c * hpq - 1, 0), col)),
                  pl.BlockSpec((halo, SSD_XBC), lambda c: (jnp.minimum((c + 1) * hpq, last_halo), col)),
                  pl.BlockSpec(conv_w.shape, lambda c: (0, 0)),
                  pl.BlockSpec(conv_b.shape, lambda c: (0, 0))],
        out_specs=[pl.BlockSpec((q, SSD_DIM), lambda c: (c, 0)), pl.BlockSpec((q, 2 * SSD_BC), lambda c: (c, 0))],
        out_shape=[jax.ShapeDtypeStruct((n, SSD_DIM), F32), jax.ShapeDtypeStruct((n, 2 * SSD_BC), BF16)],
        scratch_shapes=[pltpu.VMEM((q + 2 * halo, SSD_XBC), F32)],
        compiler_params=_params("parallel"),
        name="ssd_conv",
    )(p, p, p, conv_w, conv_b)


def _ssd_kernel(*refs, rev, finish):
    if finish:
        (xs_ref, bc_ref, dt_ref, z_ref, yb_ref, bias_ref, a_ref, ex_ref, d_ref, ng_ref,
         o_ref, h_ref, yd_ref) = refs
    else:
        xs_ref, bc_ref, dt_ref, bias_ref, a_ref, ex_ref, o_ref, h_ref, yd_ref = refs
    q = SSD_Q

    @pl.when(pl.program_id(1) == 0)
    def _():
        h_ref[...] = jnp.zeros_like(h_ref)

    xs = xs_ref[...]
    bm = bc_ref[:, :SSD_BC]
    cm = bc_ref[:, SSD_BC:]

    raw = dt_ref[...] + bias_ref[...]
    dt = jnp.maximum(raw, 0.0) + jnp.log1p(jnp.exp(-jnp.abs(raw)))
    ad = dt * a_ref[...]
    li = lax.broadcasted_iota(jnp.int32, (q, q), 0)
    si = lax.broadcasted_iota(jnp.int32, (q, q), 1)
    keep = (si >= li) if rev else (si <= li)
    tri = keep.astype(F32)
    a_cs = jnp.dot(tri, ad, precision=HIGHEST, preferred_element_type=F32)
    a_cs_t = a_cs.T
    dt_t = dt.T
    edge = 0 if rev else q - 1
    tot = a_cs[edge:edge + 1, :]
    scal = jnp.concatenate([dt * jnp.exp(tot - a_cs), jnp.exp(a_cs)], axis=0)
    s_hi, s_lo = _split_bf16(scal)
    ex = ex_ref[...]
    scal_x = (jnp.dot(s_hi, ex, preferred_element_type=F32) + jnp.dot(s_lo, ex, preferred_element_type=F32))
    ds_x = scal_x[:q]
    do_x = scal_x[q:]
    tot_x = do_x[edge:edge + 1, :]

    xs_b = xs.astype(BF16)
    xds_b = (xs * ds_x).astype(BF16)
    rep = SSD_HEADS // SSD_GROUPS
    gw = rep * SSD_HEAD_DIM
    lane0 = SSD_HEADS if rev else 0
    nt = (((1,), (1,)), ((), ()))
    y_parts = []
    for g in range(SSD_GROUPS):
        bg = bm[:, g * SSD_STATE:(g + 1) * SSD_STATE]
        cg = cm[:, g * SSD_STATE:(g + 1) * SSD_STATE]
        cb = lax.dot_general(cg, bg, nt, preferred_element_type=F32)
        for r in range(rep):
            hd = g * rep + r
            col = a_cs[:, lane0 + hd:lane0 + hd + 1]
            row = a_cs_t[lane0 + hd:lane0 + hd + 1, :]
            dec = jnp.exp(jnp.where(keep, col - row, -jnp.inf)) * dt_t[lane0 + hd:lane0 + hd + 1, :]
            yd_ref[:, hd * SSD_HEAD_DIM:(hd + 1) * SSD_HEAD_DIM] = jnp.dot(
                (cb * dec).astype(BF16), xs_b[:, hd * SSD_HEAD_DIM:(hd + 1) * SSD_HEAD_DIM],
                preferred_element_type=F32)
        gs = slice(g * gw, (g + 1) * gw)
        h_g = h_ref[:, gs]
        y_off = jnp.dot(cg, h_g.astype(BF16), preferred_element_type=F32) * do_x[:, gs]
        bg_t = bg.astype(F32).T.astype(BF16)
        st = jnp.dot(bg_t, xds_b[:, gs], preferred_element_type=F32)
        h_ref[:, gs] = h_g * tot_x[:, gs] + st
        y_parts.append(y_off)
    y = yd_ref[...] + jnp.concatenate(y_parts, axis=1)

    if finish:
        y = y + yb_ref[...] + d_ref[...] * xs
        y = y * _silu(z_ref[...])
        nw = SSD_DIM // SSD_GROUPS
        outs = []
        for g in range(SSD_GROUPS):
            yg = y[:, g * nw:(g + 1) * nw]
            outs.append(yg * lax.rsqrt(jnp.mean(yg * yg, axis=-1, keepdims=True) + NORM_EPS))
        o_ref[...] = (jnp.concatenate(outs, axis=1) * ng_ref[...]).astype(BF16)
    else:
        o_ref[...] = y


def _ssd_pass(p, xs, bc, y_other, vecs, *, rev, n_batch, seq, ctx_len, col_dt):
    n = p.shape[0]
    q = SSD_Q
    nl, ncx = seq // q, ctx_len // q
    finish = y_other is not None
    bias, a_vec, expand, d_vec, norm_g = vecs

    def chunk(b, i):
        pos_c = (ncx - 1 - i) if rev else i
        pos_l = (nl - 1 - (i - ncx)) if rev else (i - ncx)
        return jnp.where(i < ncx, n_batch * nl + b * ncx + pos_c, b * nl + pos_l)

    own = lambda col_blk, w: pl.BlockSpec((q, w), lambda b, i: (chunk(b, i), col_blk))
    const = lambda a: pl.BlockSpec(a.shape, lambda b, i: (0, 0))
    in_specs = [own(0, SSD_DIM), own(0, 2 * SSD_BC), own(col_dt // LANES, LANES)]
    args = [xs, bc, p]
    if finish:
        in_specs += [own(COL_Z // SSD_DIM, SSD_DIM), own(0, SSD_DIM)]
        args += [p, y_other]
    tail = [bias, a_vec, expand] + ([d_vec, norm_g] if finish else [])
    in_specs += [const(a) for a in tail]
    args += tail
    return pl.pallas_call(
        functools.partial(_ssd_kernel, rev=rev, finish=finish),
        grid=(n_batch, nl + ncx),
        in_specs=in_specs,
        out_specs=own(0, SSD_DIM),
        out_shape=jax.ShapeDtypeStruct((n, SSD_DIM), BF16 if finish else F32),
        scratch_shapes=[pltpu.VMEM((SSD_STATE, SSD_DIM), F32),
                        pltpu.VMEM((q, SSD_DIM), F32)],
        compiler_params=_params("parallel", "arbitrary"),
        name="ssd_bwd" if rev else "ssd_fwd",
    )(*args)


def _ssd(p, conv_w, conv_b, dt_bias, a_log, d_skip, norm_g, *, n_batch, seq, ctx_len, col_dt):
    cw = jnp.zeros((SUBLANES, SSD_XBC), F32).at[:SSD_CONV].set(conv_w.astype(F32))
    cb = conv_b.astype(F32).reshape(1, SSD_XBC)
    bias = jnp.zeros((1, LANES), F32).at[0, :2 * SSD_HEADS].set(dt_bias.astype(F32).reshape(-1))
    a = -jnp.exp(a_log.astype(F32))
    d_vec = jnp.repeat(d_skip.astype(F32), SSD_HEAD_DIM).reshape(1, SSD_DIM)
    ng = norm_g.astype(F32).reshape(1, SSD_DIM)
    head_of_channel = jnp.arange(SSD_DIM) // SSD_HEAD_DIM
    xs, bc = _ssd_conv(p, cw, cb, n_batch=n_batch, seq=seq, ctx_len=ctx_len)
    outs = {}
    for rev in (True, False):
        lane0 = SSD_HEADS if rev else 0
        a_vec = jnp.zeros((1, LANES), F32).at[0, lane0:lane0 + SSD_HEADS].set(a[1 if rev else 0])
        expand = (jnp.arange(LANES)[:, None] == (head_of_channel[None, :] + lane0)).astype(BF16)
        outs[rev] = _ssd_pass(p, xs, bc, None if rev else outs[True], (bias, a_vec, expand, d_vec, ng),
                              rev=rev, n_batch=n_batch, seq=seq, ctx_len=ctx_len, col_dt=col_dt)
    return outs[False]


def _s5_glu_kernel(y_ref, u_ref, d_ref, w_ref, o_ref):
    y = y_ref[...] + d_ref[...] * u_ref[...]
    g = 0.5 * y * (1.0 + jnp.tanh(np.sqrt(2.0 / np.pi).astype(np.float32) * (y + 0.044715 * (y * y * y))))
    gate = _sigmoid(jnp.dot(w_ref[...], g.astype(BF16), preferred_element_type=F32))
    o_ref[...] = (g * gate).T.astype(BF16)


def _s5_glu(y_t, u_t, d_s5, w_glu_t, *, n_rows):
    tm = ROW_TILE
    cm = pl.BlockSpec((S5_DIM, tm), lambda i: (0, i))
    return pl.pallas_call(
        _s5_glu_kernel,
        grid=(n_rows // tm,),
        in_specs=[cm, cm,
                  pl.BlockSpec((S5_DIM, 1), lambda i: (0, 0)),
                  pl.BlockSpec((S5_DIM, S5_DIM), lambda i: (0, 0))],
        out_specs=pl.BlockSpec((tm, S5_DIM), lambda i: (i, 0)),
        out_shape=jax.ShapeDtypeStruct((n_rows, S5_DIM), BF16),
        compiler_params=_params("parallel"),
        name="s5_glu",
    )(y_t, u_t, d_s5, w_glu_t)


def _merge_kernel(h_ref, mod_ref, oa_ref, os_ref, od_ref, g0_ref, g1_ref, g2_ref,
                  wa_ref, ws_ref, wd_ref, wo_ref, o_ref, acc_ref):
    j = pl.program_id(1)

    @pl.when(j == 0)
    def _():
        acc_ref[...] = jnp.zeros_like(acc_ref)

    m = (_sigmoid(g0_ref[...]) * jnp.dot(oa_ref[...], wa_ref[...], preferred_element_type=F32)
         + _sigmoid(g1_ref[...]) * jnp.dot(os_ref[...], ws_ref[...], preferred_element_type=F32)
         + _sigmoid(g2_ref[...]) * jnp.dot(od_ref[...], wd_ref[...], preferred_element_type=F32))
    acc_ref[...] += jnp.dot(m.astype(BF16), wo_ref[...], preferred_element_type=F32)

    @pl.when(j == pl.num_programs(1) - 1)
    def _():
        o_ref[...] = h_ref[...] + mod_ref[5:6, :] * acc_ref[...]


def _merge(h, mod, p, oa, os5, ossd, w_a, w_s, w_d, w_o, *, n_rows, tiles_per_batch, n_batch):
    d = h.shape[1]
    tm, tc = ROW_TILE, 512
    nj = d // tc
    midx = lambda i, j: (jnp.minimum(i // tiles_per_batch, n_batch), 0, 0)
    row = lambda w: pl.BlockSpec((tm, w), lambda i, j: (i, 0))
    gate = lambda br: pl.BlockSpec((tm, tc), lambda i, j: (i, (COL_GATE + br * d) // tc + j))
    wcol = lambda k: pl.BlockSpec((k, tc), lambda i, j: (0, j))
    return pl.pallas_call(
        _merge_kernel,
        grid=(n_rows // tm, nj),
        in_specs=[row(d), pl.BlockSpec((None, N_MOD, d), midx),
                  row(Q_DIM), row(S5_DIM), row(SSD_DIM), gate(0), gate(1), gate(2),
                  wcol(Q_DIM), wcol(S5_DIM), wcol(SSD_DIM),
                  pl.BlockSpec((tc, d), lambda i, j: (j, 0))],
        out_specs=row(d),
        out_shape=jax.ShapeDtypeStruct((n_rows, d), F32),
        scratch_shapes=[pltpu.VMEM((tm, d), F32)],
        compiler_params=_params("parallel", "arbitrary"),
        name="merge",
    )(h, mod, oa, os5, ossd, p, p, p, w_a, w_s, w_d, w_o)


def _rope_tables(seq, extra_rows):
    rows = seq // GRID_W
    row = jnp.repeat(jnp.arange(rows, dtype=F32), GRID_W)
    col = jnp.tile(jnp.arange(GRID_W, dtype=F32), rows)
    axis_dim = HEAD_DIM // 2
    inv_freq = ROPE_BASE ** (-jnp.arange(0, axis_dim, 2, dtype=F32) / axis_dim)
    ang_r = row[:, None] * inv_freq[None, :]
    ang_c = col[:, None] * inv_freq[None, :]
    cos = jnp.concatenate([jnp.cos(ang_r), jnp.cos(ang_c)] * 2, axis=1)
    sin = jnp.concatenate([-jnp.sin(ang_r), -jnp.sin(ang_c), jnp.sin(ang_r), jnp.sin(ang_c)], axis=1)
    cos = jnp.concatenate([cos, jnp.ones((extra_rows, HEAD_DIM), F32)], axis=0)
    sin = jnp.concatenate([sin, jnp.zeros((extra_rows, HEAD_DIM), F32)], axis=0)
    return cos, sin


def _pair_major(t):
    lead = t.shape[:-1]
    t = t.reshape(lead + (-1, 2, 2, HEAD_DIM // 4))
    return jnp.swapaxes(t, -3, -2).reshape(lead + (-1,))


def _permute_w_in(w, d):
    v0 = Q_DIM + KV_DIM
    s0 = Q_DIM + 2 * KV_DIM
    z0 = s0 + S5_DIM
    x0 = z0 + SSD_DIM
    t0 = x0 + SSD_XBC
    g0 = t0 + 2 * SSD_HEADS
    g1 = g0 + N_BRANCH * d
    parts = [_pair_major(w[:, :v0]), w[:, v0:s0], w[:, x0:t0], w[:, t0:g0],
             jnp.zeros((w.shape[0], DT_SLOT - 2 * SSD_HEADS), w.dtype), w[:, z0:x0], w[:, g0:g1]]
    return jnp.concatenate(parts, axis=1).astype(BF16), w[:, s0:z0].T.astype(BF16)


def kernel(x, c, ctx, c_ctx, w_ada, b_ada, norm_g, w_ffn_gate, w_ffn_up, w_ffn_down, w_in, qk_norm_g, attn_sink,
           w_attn_o, s5_a_re, s5_a_im, s5_log_step, s5_b_re, s5_b_im, s5_c_re, s5_c_im, s5_d, s5_w_glu, w_s5_o,
           ssd_conv_w, ssd_conv_b, ssd_dt_bias, ssd_a_log, ssd_d, ssd_norm_g, w_ssd_o, w_out):
    n_batch, seq, d = x.shape
    ctx_len = ctx.shape[1]
    depth = w_ada.shape[0]
    n_lat, n_ctx = n_batch * seq, n_batch * ctx_len
    n = n_lat + n_ctx
    assert seq % ROW_TILE == 0 and n_ctx % ROW_TILE == 0 and ctx_len % SSD_Q == 0
    assert (COL_GATE + N_BRANCH * d) % PROJ_TN == 0 and n_batch + 1 <= SUBLANES

    h, h_ctx = x.reshape(n_lat, d), ctx.reshape(n_ctx, d)
    c_rows = jnp.zeros((SUBLANES, d), F32).at[:n_batch].set(c.astype(F32)).at[n_batch].set(c_ctx.astype(F32))
    mods = _ada(c_rows, w_ada, b_ada)[:, :n_batch + 1].reshape(depth, n_batch + 1, N_MOD, d)
    cos_t, sin_t = _rope_tables(seq, ROW_TILE)

    tpb = seq // ROW_TILE
    for l in range(depth):
        last = l == depth - 1
        ffn_w = lambda k: (w_ffn_gate[l, k].astype(BF16), w_ffn_up[l, k].astype(BF16), w_ffn_down[l, k].astype(BF16))
        g = norm_g[l].astype(F32)
        h = _ffn(h, mods[l], g, *ffn_w(0), rows=(0, 1, 2, 0), n_rows=n, tiles_per_batch=tpb, n_batch=n_batch,
                 h_ctx=h_ctx if l == 0 else None)
        p, u_t = _proj(h, mods[l], g, *_permute_w_in(w_in[l], d), tiles_per_batch=tpb, n_batch=n_batch)

        qkv = _prep(p, cos_t, sin_t, _pair_major(qk_norm_g[l].astype(F32)), tiles_lat=n_lat // ROW_TILE,
                    tiles_per_batch=tpb)
        oa = _attention(qkv, attn_sink[l].astype(F32), n_batch=n_batch, seq=seq, ctx_len=ctx_len, with_ctx=not last)

        tables = _s5_tables(s5_a_re[l], s5_a_im[l], s5_log_step[l], s5_b_re[l], s5_b_im[l], s5_c_re[l], s5_c_im[l])
        y_t = _s5(u_t, tables, n_batch=n_batch, seq=seq, ctx_len=ctx_len)

        ossd = _ssd(p, ssd_conv_w[l], ssd_conv_b[l], ssd_dt_bias[l], ssd_a_log[l], ssd_d[l], ssd_norm_g[l],
                    n_batch=n_batch, seq=seq, ctx_len=ctx_len, col_dt=COL_DT)

        n_out = n_lat if last else n
        os5 = _s5_glu(y_t, u_t, s5_d[l].astype(F32).reshape(S5_DIM, 1),
                      s5_w_glu[l].T.astype(BF16), n_rows=n_out)
        h = _merge(h, mods[l], p, oa, os5, ossd,
                   w_attn_o[l].astype(BF16), w_s5_o[l].astype(BF16), w_ssd_o[l].astype(BF16), w_out[l].astype(BF16),
                   n_rows=n_out, tiles_per_batch=tpb, n_batch=n_batch)
        h = _ffn(h, mods[l], g, *ffn_w(1), rows=(6, 7, 8, 2), n_rows=n_out, tiles_per_batch=tpb, n_batch=n_batch)
    return h.reshape(n_batch, seq, d)
```

```python
import functools

import numpy as np
import jax
import jax.numpy as jnp
from jax import lax
from jax.experimental import pallas as pl
from jax.experimental.pallas import tpu as pltpu

F32 = jnp.float32
BF16 = jnp.bfloat16
HIGHEST = lax.Precision.HIGHEST

GRID_W = 64
N_MOD = 9
NORM_EPS = 1e-6
ATTN_HEADS = 12
ATTN_KV_HEADS = 4
HEAD_DIM = 128
ATTN_BLOCK = 128
ROPE_BASE = 10000.0
S5_DIM = 1024
S5_GROUP = 16
S5_GROUPS = S5_DIM // S5_GROUP
S5_STATE = 64
SSD_DIM = 1536
SSD_HEAD_DIM = 64
SSD_HEADS = SSD_DIM // SSD_HEAD_DIM
SSD_GROUPS = 4
SSD_STATE = 128
SSD_CONV = 5
N_BRANCH = 3
Q_DIM = ATTN_HEADS * HEAD_DIM
KV_DIM = ATTN_KV_HEADS * HEAD_DIM
SSD_BC = SSD_GROUPS * SSD_STATE
SSD_XBC = SSD_DIM + 2 * SSD_BC

VMEM_LIMIT_BYTES = 56 * 1024 * 1024
LANES = 128
SUBLANES = 8

COL_QKV = 0
COL_XBC = Q_DIM + 2 * KV_DIM
COL_DT = COL_XBC + SSD_XBC
DT_SLOT = 1024
COL_Z = COL_DT + DT_SLOT
COL_GATE = COL_Z + SSD_DIM
PROJ_TN = 1536

ROW_TILE = 512
S5_T = 128
SSD_Q = 128


def _params(*sem):
    return pltpu.CompilerParams(dimension_semantics=sem, vmem_limit_bytes=VMEM_LIMIT_BYTES)


def _sigmoid(v):
    return 0.5 * (1.0 + jnp.tanh(0.5 * v))


def _silu(v):
    return v * _sigmoid(v)


def _split_bf16(v):
    hi = v.astype(BF16)
    return hi, (v - hi.astype(F32)).astype(BF16)


def _rms_mod(x, g, shift, scale):
    y = x * lax.rsqrt(jnp.mean(x * x, axis=-1, keepdims=True) + NORM_EPS) * g
    return y * (1.0 + scale) + shift


def _ada_kernel(c_ref, w_ref, b_ref, o_ref):
    s = _silu(c_ref[...]).astype(BF16)
    o_ref[...] = jnp.dot(s, w_ref[...].astype(BF16), preferred_element_type=F32) + b_ref[...]


def _ada(c_rows, w_ada, b_ada):
    depth, d, n = w_ada.shape
    tn = 1024
    return pl.pallas_call(
        _ada_kernel,
        grid=(depth, n // tn),
        in_specs=[pl.BlockSpec((SUBLANES, d), lambda l, j: (0, 0)),
                  pl.BlockSpec((None, d, tn), lambda l, j: (l, 0, j)),
                  pl.BlockSpec((None, 1, tn), lambda l, j: (l, 0, j))],
        out_specs=pl.BlockSpec((None, SUBLANES, tn), lambda l, j: (l, 0, j)),
        out_shape=jax.ShapeDtypeStruct((depth, SUBLANES, n), F32),
        compiler_params=_params("parallel", "parallel"),
        name="ada_mod",
    )(c_rows, w_ada, b_ada.reshape(depth, 1, n))


def _ffn_kernel(*refs, rows, tiles_lat):
    r_shift, r_scale, r_gate, r_norm = rows
    j = pl.program_id(1)
    if tiles_lat is None:
        h_ref, mod_ref, g_ref, wg_ref, wu_ref, wd_ref, o_ref, xn_ref, acc_ref = refs
    else:
        lat_ref, ctx_ref, mod_ref, g_ref, wg_ref, wu_ref, wd_ref, o_ref, xn_ref, acc_ref, h_ref = refs
        is_lat = pl.program_id(0) < tiles_lat

        @pl.when(jnp.logical_and(j == 0, is_lat))
        def _():
            h_ref[...] = lat_ref[...]

        @pl.when(jnp.logical_and(j == 0, jnp.logical_not(is_lat)))
        def _():
            h_ref[...] = ctx_ref[...]

    @pl.when(j == 0)
    def _():
        xn = _rms_mod(h_ref[...], g_ref[r_norm:r_norm + 1, :], mod_ref[r_shift:r_shift + 1, :],
                      mod_ref[r_scale:r_scale + 1, :])
        xn_ref[...] = xn.astype(BF16)
        acc_ref[...] = jnp.zeros_like(acc_ref)

    xn = xn_ref[...]
    a = jnp.dot(xn, wg_ref[...], preferred_element_type=F32)
    b = jnp.dot(xn, wu_ref[...], preferred_element_type=F32)
    t = (_silu(a) * b).astype(BF16)
    acc_ref[...] += jnp.dot(t, wd_ref[...], preferred_element_type=F32)

    @pl.when(j == pl.num_programs(1) - 1)
    def _():
        o_ref[...] = h_ref[...] + 0.5 * mod_ref[r_gate:r_gate + 1, :] * acc_ref[...]


def _ffn(h, mod, g, wg, wu, wd, *, rows, n_rows, tiles_per_batch, n_batch, h_ctx=None):
    d = h.shape[1]
    f = wg.shape[1]
    tm, tf = ROW_TILE, 512
    midx = lambda i, j: (jnp.minimum(i // tiles_per_batch, n_batch), 0, 0)
    scratch = [pltpu.VMEM((tm, d), BF16), pltpu.VMEM((tm, d), F32)]
    if h_ctx is None:
        tiles_lat = None
        srcs, src_specs = [h], [pl.BlockSpec((tm, d), lambda i, j: (i, 0))]
    else:
        tiles_lat = h.shape[0] // tm
        srcs = [h, h_ctx]
        src_specs = [pl.BlockSpec((tm, d), lambda i, j: (jnp.minimum(i, tiles_lat - 1), 0)),
                     pl.BlockSpec((tm, d), lambda i, j: (jnp.maximum(i - tiles_lat, 0), 0))]
        scratch.append(pltpu.VMEM((tm, d), F32))
    return pl.pallas_call(
        functools.partial(_ffn_kernel, rows=rows, tiles_lat=tiles_lat),
        grid=(n_rows // tm, f // tf),
        in_specs=src_specs + [
                  pl.BlockSpec((None, N_MOD, d), midx),
                  pl.BlockSpec(g.shape, lambda i, j: (0, 0)),
                  pl.BlockSpec((d, tf), lambda i, j: (0, j)),
                  pl.BlockSpec((d, tf), lambda i, j: (0, j)),
                  pl.BlockSpec((tf, d), lambda i, j: (j, 0))],
        out_specs=pl.BlockSpec((tm, d), lambda i, j: (i, 0)),
        out_shape=jax.ShapeDtypeStruct((n_rows, d), F32),
        scratch_shapes=scratch,
        compiler_params=_params("parallel", "arbitrary"),
        name="ffn",
    )(*srcs, mod, g, wg, wu, wd)


def _proj_kernel(h_ref, mod_ref, g_ref, w_ref, ws_ref, o_ref, ut_ref, u_ref):
    @pl.when(pl.program_id(1) == 0)
    def _():
        u = _rms_mod(h_ref[...], g_ref[1:2, :], mod_ref[3:4, :], mod_ref[4:5, :]).astype(BF16)
        u_ref[...] = u
        ut_ref[...] = lax.dot_general(ws_ref[...], u, (((1,), (1,)), ((), ())), preferred_element_type=F32)

    o_ref[...] = jnp.dot(u_ref[...], w_ref[...], preferred_element_type=F32)


def _proj(h, mod, g, w, ws_t, *, tiles_per_batch, n_batch):
    n, d = h.shape
    ncol = w.shape[1]
    tm, tn = ROW_TILE, PROJ_TN
    midx = lambda i, j: (jnp.minimum(i // tiles_per_batch, n_batch), 0, 0)
    return pl.pallas_call(
        _proj_kernel,
        grid=(n // tm, ncol // tn),
        in_specs=[pl.BlockSpec((tm, d), lambda i, j: (i, 0)),
                  pl.BlockSpec((None, N_MOD, d), midx),
                  pl.BlockSpec(g.shape, lambda i, j: (0, 0)),
                  pl.BlockSpec((d, tn), lambda i, j: (0, j)),
                  pl.BlockSpec(ws_t.shape, lambda i, j: (0, 0))],
        out_specs=[pl.BlockSpec((tm, tn), lambda i, j: (i, j)),
                   pl.BlockSpec((S5_DIM, tm), lambda i, j: (0, i))],
        out_shape=[jax.ShapeDtypeStruct((n, ncol), F32), jax.ShapeDtypeStruct((S5_DIM, n), F32)],
        scratch_shapes=[pltpu.VMEM((tm, d), BF16)],
        compiler_params=_params("parallel", "arbitrary"),
        name="in_proj",
    )(h, mod, g, w, ws_t)


def _prep_kernel(p_ref, cos_ref, sin_ref, g_ref, o_ref):
    cos = cos_ref[...]
    sin = sin_ref[...]
    scale = HEAD_DIM ** -0.5
    mean_mat = jnp.full((HEAD_DIM, HEAD_DIM), 1.0 / HEAD_DIM, BF16)
    for hd in range(ATTN_HEADS + ATTN_KV_HEADS):
        sl = slice(hd * HEAD_DIM, (hd + 1) * HEAD_DIM)
        x = p_ref[:, sl]
        g = g_ref[0:1, :] if hd < ATTN_HEADS else g_ref[1:2, :]
        sq_hi, sq_lo = _split_bf16(x * x)
        ms = (jnp.dot(sq_hi, mean_mat, preferred_element_type=F32)
              + jnp.dot(sq_lo, mean_mat, preferred_element_type=F32))
        y = x * lax.rsqrt(ms + NORM_EPS) * g
        y = y * cos + pltpu.roll(y, HEAD_DIM // 2, 1) * sin
        if hd < ATTN_HEADS:
            y = y * scale
        o_ref[:, sl] = y.astype(BF16)
    vs = slice(Q_DIM + KV_DIM, Q_DIM + 2 * KV_DIM)
    o_ref[:, vs] = p_ref[:, vs].astype(BF16)


def _prep(p, cos_t, sin_t, qk_g, *, tiles_lat, tiles_per_batch):
    n = p.shape[0]
    tm = ROW_TILE
    w = Q_DIM + 2 * KV_DIM
    tidx = lambda i: (jnp.where(i < tiles_lat, i % tiles_per_batch, tiles_per_batch), 0)
    return pl.pallas_call(
        _prep_kernel,
        grid=(n // tm,),
        in_specs=[pl.BlockSpec((tm, w), lambda i: (i, 0)),
                  pl.BlockSpec((tm, HEAD_DIM), tidx),
                  pl.BlockSpec((tm, HEAD_DIM), tidx),
                  pl.BlockSpec(qk_g.shape, lambda i: (0, 0))],
        out_specs=pl.BlockSpec((tm, w), lambda i: (i, 0)),
        out_shape=jax.ShapeDtypeStruct((n, w), BF16),
        compiler_params=_params("parallel"),
        name="qk_prep",
    )(p, cos_t, sin_t, qk_g)


def _attn_kernel(sink_ref, q_ref, kp_ref, ko_ref, kn_ref, vp_ref, vo_ref, vn_ref, kc_ref, vc_ref, o_ref, *, nb):
    n = pl.program_id(1)
    blk = ATTN_BLOCK
    grp = ATTN_HEADS // ATTN_KV_HEADS
    is_lat = n < nb
    has_prev = jnp.logical_and(is_lat, n > 0)
    has_next = jnp.logical_and(is_lat, n < nb - 1)
    rows = grp * blk
    r = lax.broadcasted_iota(jnp.int32, (rows, 3 * blk), 0)
    kk = lax.broadcasted_iota(jnp.int32, (rows, 3 * blk), 1)
    i = r & (blk - 1)
    far = 4 * blk
    off_prev = jnp.where(has_prev, 0, far)
    off_own = jnp.where(is_lat, 0, far)
    off_next = jnp.where(has_next, 0, far)
    mask = (((kk < blk) & (kk >= i + off_prev))
            | ((kk >= blk + off_own) & (kk < 2 * blk))
            | ((kk >= 2 * blk) & (kk - 2 * blk <= i - off_next)))
    rr = lax.broadcasted_iota(jnp.int32, (rows, 1), 0)
    nt = (((1,), (1,)), ((), ()))
    for kv in range(ATTN_KV_HEADS):
        ks = slice(kv * HEAD_DIM, (kv + 1) * HEAD_DIM)
        qh = jnp.concatenate([q_ref[:, (kv * grp + g) * HEAD_DIM:(kv * grp + g + 1) * HEAD_DIM] for g in range(grp)],
                             axis=0)
        kl = jnp.concatenate([kp_ref[:, ks], ko_ref[:, ks], kn_ref[:, ks]], axis=0)
        vl = jnp.concatenate([vp_ref[:, ks], vo_ref[:, ks], vn_ref[:, ks]], axis=0)
        s_loc = lax.dot_general(qh, kl, nt, preferred_element_type=F32)
        s_ctx = lax.dot_general(qh, kc_ref[:, ks], nt, preferred_element_type=F32)
        s_loc = jnp.where(mask, s_loc, -jnp.inf)
        sink = jnp.full((rows, 1), sink_ref[kv * grp], F32)
        for g in range(1, grp):
            sink = jnp.where(rr >= g * blk, sink_ref[kv * grp + g], sink)
        m = jnp.maximum(jnp.maximum(jnp.max(s_loc, axis=-1, keepdims=True),
                                    jnp.max(s_ctx, axis=-1, keepdims=True)), sink)
        p_loc = jnp.exp(s_loc - m)
        p_ctx = jnp.exp(s_ctx - m)
        denom = (jnp.sum(p_loc, axis=-1, keepdims=True) + jnp.sum(p_ctx, axis=-1, keepdims=True)
                 + jnp.exp(sink - m))
        o = (jnp.dot(p_loc.astype(BF16), vl, preferred_element_type=F32)
             + jnp.dot(p_ctx.astype(BF16), vc_ref[:, ks], preferred_element_type=F32))
        o = o * (1.0 / denom)
        for g in range(grp):
            hs = slice((kv * grp + g) * HEAD_DIM, (kv * grp + g + 1) * HEAD_DIM)
            o_ref[:, hs] = o[g * blk:(g + 1) * blk, :].astype(BF16)


def _attention(qkv, sink, *, n_batch, seq, ctx_len, with_ctx):
    n = qkv.shape[0]
    blk = ATTN_BLOCK
    nb = seq // blk
    ncb = ctx_len // blk
    steps = nb + (ncb if with_ctx else 0)
    ctx0 = n_batch * nb
    kcol = Q_DIM // KV_DIM

    def own(b, s):
        return jnp.where(s < nb, b * nb + s, ctx0 + b * ncb + (s - nb))

    def prev(b, s):
        return jnp.where(s < nb, b * nb + jnp.maximum(s - 1, 0), ctx0 + b * ncb)

    def nxt(b, s):
        return jnp.where(s < nb, b * nb + jnp.minimum(s + 1, nb - 1), ctx0 + b * ncb)

    ctx_blk = lambda b, s: ((n_batch * seq) // ctx_len + b)
    kv_spec = lambda f, col: pl.BlockSpec((blk, KV_DIM), lambda b, s: (f(b, s), col))
    return pl.pallas_call(
        functools.partial(_attn_kernel, nb=nb),
        grid=(n_batch, steps),
        in_specs=[pl.BlockSpec(memory_space=pltpu.SMEM),
                  pl.BlockSpec((blk, Q_DIM), lambda b, s: (own(b, s), 0)),
                  kv_spec(prev, kcol), kv_spec(own, kcol), kv_spec(nxt, kcol),
                  kv_spec(prev, kcol + 1), kv_spec(own, kcol + 1), kv_spec(nxt, kcol + 1),
                  pl.BlockSpec((ctx_len, KV_DIM), lambda b, s: (ctx_blk(b, s), kcol)),
                  pl.BlockSpec((ctx_len, KV_DIM), lambda b, s: (ctx_blk(b, s), kcol + 1))],
        out_specs=pl.BlockSpec((blk, Q_DIM), lambda b, s: (own(b, s), 0)),
        out_shape=jax.ShapeDtypeStruct((n if with_ctx else n_batch * seq, Q_DIM), BF16),
        compiler_params=_params("parallel", "parallel"),
        name="window_attn",
    )(sink, qkv, qkv, qkv, qkv, qkv, qkv, qkv, qkv, qkv)


def _s5_tables(a_re, a_im, log_step, b_re, b_im, c_re, c_im):
    t = S5_T
    step = jnp.exp(log_step.astype(F32))[..., None]
    a_re = a_re.astype(F32)
    a_im = a_im.astype(F32)
    k = jnp.arange(t + 1, dtype=F32)[:, None, None, None]
    mag = jnp.exp(k * (a_re * step)[None])
    ang = k * (a_im * step)[None]
    pr, pi = mag * jnp.cos(ang), mag * jnp.sin(ang)
    den = a_re * a_re + a_im * a_im
    nr, ni = pr[1] - 1.0, pi[1]
    wr = (nr * a_re + ni * a_im) / den
    wi = (ni * a_re - nr * a_im) / den
    b_re = b_re.astype(F32)[None]
    b_im = b_im.astype(F32)[None]
    wbr = wr[..., None] * b_re - wi[..., None] * b_im
    wbi = wr[..., None] * b_im + wi[..., None] * b_re
    c_re = c_re.astype(F32)
    c_im = c_im.astype(F32)
    grp = S5_GROUP

    wbr_t = jnp.swapaxes(wbr, 2, 3)[:, :, :, None, :]
    wbi_t = jnp.swapaxes(wbi, 2, 3)[:, :, :, None, :]
    qr = (c_re[None, :, None] * wbr_t - c_im[None, :, None] * wbi_t).reshape(2, S5_GROUPS, grp * grp, S5_STATE)
    qi = (c_re[None, :, None] * wbi_t + c_im[None, :, None] * wbr_t).reshape(2, S5_GROUPS, grp * grp, S5_STATE)

    def taps(d, pw_r, pw_i):
        return (jnp.einsum('gmp,kgp->gmk', qr[d], pw_r, precision=HIGHEST)
                - jnp.einsum('gmp,kgp->gmk', qi[d], pw_i, precision=HIGHEST))

    kf = taps(0, pr[:t, 0], pi[:t, 0])
    kf = kf.at[:, :, 0].add(jnp.sum(qr[1], axis=-1))
    kb = taps(1, pr[t:0:-1, 1], pi[t:0:-1, 1])

    pw_f = jnp.concatenate([pr[t - 1::-1, 0], pr[:t, 1], pi[t - 1::-1, 0], pi[:t, 1]], axis=-1)
    pw_f = jnp.transpose(pw_f, (1, 0, 2))
    wb = jnp.concatenate([wbr[0], wbr[1], wbi[0], wbi[1]], axis=1)
    wb = jnp.swapaxes(wb, 1, 2)
    pw_e = jnp.concatenate([pr[1:t + 1, 0], pr[t:0:-1, 1], pi[1:t + 1, 0], pi[t:0:-1, 1]], axis=-1)
    pw_e = jnp.transpose(pw_e, (1, 2, 0))
    c_tab = jnp.concatenate([jnp.swapaxes(c_re, 1, 2), jnp.swapaxes(c_im, 1, 2)], axis=2)
    c_tab = jnp.concatenate([c_tab, c_tab], axis=1)
    dec_re = jnp.concatenate([pr[t, 0], pr[t, 1]], axis=-1)
    dec_im = jnp.concatenate([pi[t, 0], pi[t, 1]], axis=-1)
    return kf, kb, pw_f, wb, pw_e, c_tab, dec_re, dec_im


def _s5_chunk_rows(u_ref):
    nc = u_ref.shape[1] // S5_T
    x = pltpu.einshape("icl->cil", u_ref[...].reshape(S5_GROUP, nc, S5_T))
    return x.reshape(nc, S5_GROUP * S5_T).astype(BF16)


def _s5_state_kernel(u_ref, pw_ref, wb_ref, sre_ref, sim_ref, f_ref):
    half = 2 * S5_STATE
    pw_re, pw_im = pw_ref[:, :half], pw_ref[:, half:]
    for i in range(S5_GROUP):
        wb_re, wb_im = wb_ref[i:i + 1, :half], wb_ref[i:i + 1, half:]
        blk = jnp.concatenate([pw_re * wb_re - pw_im * wb_im, pw_re * wb_im + pw_im * wb_re], axis=1)
        f_ref[i * S5_T:(i + 1) * S5_T, :] = blk.astype(BF16)
    s = jnp.dot(_s5_chunk_rows(u_ref), f_ref[...], preferred_element_type=F32)
    sre_ref[...] = s[:, :2 * S5_STATE]
    sim_ref[...] = s[:, 2 * S5_STATE:]


def _s5_rec_kernel(sre_ref, sim_ref, are_ref, aim_ref, hre_ref, him_ref, *, n_batch, nlat, nctx):
    are = are_ref[...]
    aim = aim_ref[...]
    fwd = lax.broadcasted_iota(jnp.int32, are.shape, 1) < S5_STATE
    half = S5_STATE

    def body(i, carry):
        out = []
        for b in range(n_batch):
            hr, hi = carry[b]
            lat0 = b * nlat
            ctx0 = n_batch * nlat + b * nctx
            in_ctx = i < nctx
            fi = jnp.where(in_ctx, ctx0 + i, lat0 + i - nctx)
            bi = jnp.where(in_ctx, ctx0 + nctx - 1 - i, lat0 + nlat - 1 - (i - nctx))
            hre_ref[fi, :, 0:half] = hr[:, 0:half]
            hre_ref[bi, :, half:] = hr[:, half:]
            him_ref[fi, :, 0:half] = hi[:, 0:half]
            him_ref[bi, :, half:] = hi[:, half:]
            sr = jnp.where(fwd, sre_ref[fi], sre_ref[bi])
            si = jnp.where(fwd, sim_ref[fi], sim_ref[bi])
            out.append((are * hr - aim * hi + sr, are * hi + aim * hr + si))
        return tuple(out)

    zero = jnp.zeros(are.shape, F32)
    lax.fori_loop(0, nlat + nctx, body, tuple((zero, zero) for _ in range(n_batch)))


def _s5_out_kernel(u_ref, hre_ref, him_ref, kf_ref, kb_ref, pw_ref, c_ref, y_ref, m_ref, e_ref):
    t = S5_T
    half = 2 * S5_STATE
    pw_re, pw_im = pw_ref[:half, :], pw_ref[half:, :]
    for o in range(S5_GROUP):
        c_re = c_ref[:, o:o + 1]
        c_im = c_ref[:, S5_GROUP + o:S5_GROUP + o + 1]
        blk = jnp.concatenate([c_re * pw_re - c_im * pw_im, -(c_re * pw_im + c_im * pw_re)], axis=0)
        e_ref[:, o * t:(o + 1) * t] = blk.astype(BF16)
    s_idx = lax.broadcasted_iota(jnp.int32, (t, t), 0)
    t_idx = lax.broadcasted_iota(jnp.int32, (t, t), 1)
    causal = t_idx >= s_idx

    def fill(i, carry):
        r0 = pl.multiple_of(i * S5_GROUP, S5_GROUP)
        kf = kf_ref[pl.ds(r0, S5_GROUP), :]
        kb = kb_ref[pl.ds(r0, S5_GROUP), :]
        row0 = pl.multiple_of(i * t, t)
        for o in range(S5_GROUP):
            fwd = pltpu.roll(jnp.broadcast_to(kf[o:o + 1, :], (t, t)), 0, 1, stride=1, stride_axis=0)
            bwd = pltpu.roll(jnp.broadcast_to(kb[o:o + 1, :], (t, t)), 0, 1, stride=1, stride_axis=0)
            m_ref[pl.ds(row0, t), o * t:(o + 1) * t] = jnp.where(causal, fwd, bwd).astype(BF16)
        return carry

    lax.fori_loop(0, S5_GROUP, fill, 0)
    y = jnp.dot(_s5_chunk_rows(u_ref), m_ref[...], preferred_element_type=F32)
    hin = jnp.concatenate([hre_ref[...], him_ref[...]], axis=1).astype(BF16)
    y = y + jnp.dot(hin, e_ref[...], preferred_element_type=F32)
    nc = y.shape[0]
    y = pltpu.einshape("cil->icl", y.reshape(nc, S5_GROUP, t))
    y_ref[...] = y.reshape(S5_GROUP, nc * t)


def _s5(u_t, tables, *, n_batch, seq, ctx_len):
    kf, kb, pw_f, wb, pw_e, c_tab, dec_re, dec_im = tables
    g = S5_GROUPS
    n = u_t.shape[1]
    nc = n // S5_T
    tw = S5_GROUP * S5_T
    sw = 2 * S5_STATE
    gspec = lambda shape: pl.BlockSpec((None,) + shape, lambda i: (i, 0, 0))
    uspec = pl.BlockSpec((S5_GROUP, n), lambda i: (i, 0))
    s_re, s_im = pl.pallas_call(
        _s5_state_kernel,
        grid=(g,),
        in_specs=[uspec, gspec(pw_f.shape[1:]), gspec(wb.shape[1:])],
        out_specs=[pl.BlockSpec((nc, sw), lambda i: (0, i)), pl.BlockSpec((nc, sw), lambda i: (0, i))],
        out_shape=[jax.ShapeDtypeStruct((nc, g * sw), F32)] * 2,
        scratch_shapes=[pltpu.VMEM((tw, 2 * sw), BF16)],
        compiler_params=_params("parallel"),
        name="s5_chunk_state",
    )(u_t, pw_f, wb)
    gb = SUBLANES
    blk3 = pl.BlockSpec((nc, gb, sw), lambda i: (0, i, 0))
    dspec = pl.BlockSpec((gb, sw), lambda i: (i, 0))
    h_re, h_im = pl.pallas_call(
        functools.partial(_s5_rec_kernel, n_batch=n_batch, nlat=seq // S5_T, nctx=ctx_len // S5_T),
        grid=(g // gb,),
        in_specs=[blk3, blk3, dspec, dspec],
        out_specs=[blk3, blk3],
        out_shape=[jax.ShapeDtypeStruct((nc, g, sw), F32)] * 2,
        compiler_params=_params("parallel"),
        name="s5_chunk_scan",
    )(s_re.reshape(nc, g, sw), s_im.reshape(nc, g, sw), dec_re, dec_im)
    hspec = pl.BlockSpec((nc, sw), lambda i: (0, i))
    tap = gspec((S5_GROUP * S5_GROUP, S5_T))
    return pl.pallas_call(
        _s5_out_kernel,
        grid=(g,),
        in_specs=[uspec, hspec, hspec, tap, tap, gspec(pw_e.shape[1:]), gspec(c_tab.shape[1:])],
        out_specs=uspec,
        out_shape=jax.ShapeDtypeStruct(u_t.shape, F32),
        scratch_shapes=[pltpu.VMEM((tw, tw), BF16), pltpu.VMEM((2 * sw, tw), BF16)],
        compiler_params=_params("parallel"),
        name="s5_out",
    )(u_t, h_re.reshape(nc, g * sw), h_im.reshape(nc, g * sw), kf, kb, pw_e, c_tab)


def _ssd_conv_kernel(xo_ref, xp_ref, xn_ref, cw_ref, cb_ref, xs_ref, bc_ref, xe_ref, *, n_batch, n_lat_chunks,
                     n_ctx_chunks):
    q = SSD_Q
    halo = SUBLANES
    c = pl.program_id(0)
    lat_total = n_batch * n_lat_chunks
    is_lat = c < lat_total
    pos = jnp.where(is_lat, lax.rem(c, n_lat_chunks), lax.rem(jnp.maximum(c - lat_total, 0), n_ctx_chunks))
    length = jnp.where(is_lat, n_lat_chunks, n_ctx_chunks)
    has_prev = pos > 0
    has_next = pos < length - 1
    xe_ref[0:halo, :] = xp_ref[...] * has_prev.astype(F32)
    xe_ref[halo:halo + q, :] = xo_ref[...]
    xe_ref[halo + q:, :] = xn_ref[...] * has_next.astype(F32)
    xe = xe_ref[...]
    mid = SSD_CONV // 2
    acc = cb_ref[...] + cw_ref[mid:mid + 1, :] * xo_ref[...]
    for k in range(SSD_CONV):
        if k != mid:
            acc = acc + cw_ref[k:k + 1, :] * pltpu.roll(xe, (mid - k) % (q + 2 * halo), 0)[halo:halo + q]
    xbc = _silu(acc)
    xs_ref[...] = xbc[:, :SSD_DIM]
    bc_ref[...] = xbc[:, SSD_DIM:].astype(BF16)


def _ssd_conv(p, conv_w, conv_b, *, n_batch, seq, ctx_len):
    n = p.shape[0]
    q = SSD_Q
    halo = SUBLANES
    hpq = q // halo
    last_halo = n // halo - 1
    col = COL_XBC // SSD_XBC
    return pl.pallas_call(
        functools.partial(_ssd_conv_kernel, n_batch=n_batch, n_lat_chunks=seq // q, n_ctx_chunks=ctx_len // q),
        grid=(n // q,),
        in_specs=[pl.BlockSpec((q, SSD_XBC), lambda c: (c, col)),
                  pl.BlockSpec((halo, SSD_XBC), lambda c: (jnp.maximum(c * hpq - 1, 0), col)),
                  pl.BlockSpec((halo, SSD_XBC), lambda c: (jnp.minimum((c + 1) * hpq, last_halo), col)),
                  pl.BlockSpec(conv_w.shape, lambda c: (0, 0)),
                  pl.BlockSpec(conv_b.shape, lambda c: (0, 0))],
        out_specs=[pl.BlockSpec((q, SSD_DIM), lambda c: (c, 0)), pl.BlockSpec((q, 2 * SSD_BC), lambda c: (c, 0))],
        out_shape=[jax.ShapeDtypeStruct((n, SSD_DIM), F32), jax.ShapeDtypeStruct((n, 2 * SSD_BC), BF16)],
        scratch_shapes=[pltpu.VMEM((q + 2 * halo, SSD_XBC), F32)],
        compiler_params=_params("parallel"),
        name="ssd_conv",
    )(p, p, p, conv_w, conv_b)


def _ssd_kernel(*refs, rev, finish):
    if finish:
        (xs_ref, bc_ref, dt_ref, z_ref, yb_ref, bias_ref, a_ref, ex_ref, d_ref, ng_ref,
         o_ref, h_ref, yd_ref) = refs
    else:
        xs_ref, bc_ref, dt_ref, bias_ref, a_ref, ex_ref, o_ref, h_ref, yd_ref = refs
    q = SSD_Q

    @pl.when(pl.program_id(1) == 0)
    def _():
        h_ref[...] = jnp.zeros_like(h_ref)

    xs = xs_ref[...]
    bm = bc_ref[:, :SSD_BC]
    cm = bc_ref[:, SSD_BC:]

    raw = dt_ref[...] + bias_ref[...]
    dt = jnp.maximum(raw, 0.0) + jnp.log1p(jnp.exp(-jnp.abs(raw)))
    ad = dt * a_ref[...]
    li = lax.broadcasted_iota(jnp.int32, (q, q), 0)
    si = lax.broadcasted_iota(jnp.int32, (q, q), 1)
    keep = (si >= li) if rev else (si <= li)
    tri = keep.astype(F32)
    a_cs = jnp.dot(tri, ad, precision=HIGHEST, preferred_element_type=F32)
    a_cs_t = a_cs.T
    dt_t = dt.T
    edge = 0 if rev else q - 1
    tot = a_cs[edge:edge + 1, :]
    scal = jnp.concatenate([dt * jnp.exp(tot - a_cs), jnp.exp(a_cs)], axis=0)
    s_hi, s_lo = _split_bf16(scal)
    ex = ex_ref[...]
    scal_x = (jnp.dot(s_hi, ex, preferred_element_type=F32) + jnp.dot(s_lo, ex, preferred_element_type=F32))
    ds_x = scal_x[:q]
    do_x = scal_x[q:]
    tot_x = do_x[edge:edge + 1, :]

    xs_b = xs.astype(BF16)
    xds_b = (xs * ds_x).astype(BF16)
    rep = SSD_HEADS // SSD_GROUPS
    gw = rep * SSD_HEAD_DIM
    lane0 = SSD_HEADS if rev else 0
    nt = (((1,), (1,)), ((), ()))
    y_parts = []
    for g in range(SSD_GROUPS):
        bg = bm[:, g * SSD_STATE:(g + 1) * SSD_STATE]
        cg = cm[:, g * SSD_STATE:(g + 1) * SSD_STATE]
        cb = lax.dot_general(cg, bg, nt, preferred_element_type=F32)
        for r in range(rep):
            hd = g * rep + r
            col = a_cs[:, lane0 + hd:lane0 + hd + 1]
            row = a_cs_t[lane0 + hd:lane0 + hd + 1, :]
            dec = jnp.exp(jnp.where(keep, col - row, -jnp.inf)) * dt_t[lane0 + hd:lane0 + hd + 1, :]
            yd_ref[:, hd * SSD_HEAD_DIM:(hd + 1) * SSD_HEAD_DIM] = jnp.dot(
                (cb * dec).astype(BF16), xs_b[:, hd * SSD_HEAD_DIM:(hd + 1) * SSD_HEAD_DIM],
                preferred_element_type=F32)
        gs = slice(g * gw, (g + 1) * gw)
        h_g = h_ref[:, gs]
        y_off = jnp.dot(cg, h_g.astype(BF16), preferred_element_type=F32) * do_x[:, gs]
        bg_t = bg.astype(F32).T.astype(BF16)
        st = jnp.dot(bg_t, xds_b[:, gs], preferred_element_type=F32)
        h_ref[:, gs] = h_g * tot_x[:, gs] + st
        y_parts.append(y_off)
    y = yd_ref[...] + jnp.concatenate(y_parts, axis=1)

    if finish:
        y = y + yb_ref[...] + d_ref[...] * xs
        y = y * _silu(z_ref[...])
        nw = SSD_DIM // SSD_GROUPS
        outs = []
        for g in range(SSD_GROUPS):
            yg = y[:, g * nw:(g + 1) * nw]
            outs.append(yg * lax.rsqrt(jnp.mean(yg * yg, axis=-1, keepdims=True) + NORM_EPS))
        o_ref[...] = (jnp.concatenate(outs, axis=1) * ng_ref[...]).astype(BF16)
    else:
        o_ref[...] = y


def _ssd_pass(p, xs, bc, y_other, vecs, *, rev, n_batch, seq, ctx_len, col_dt):
    n = p.shape[0]
    q = SSD_Q
    nl, ncx = seq // q, ctx_len // q
    finish = y_other is not None
    bias, a_vec, expand, d_vec, norm_g = vecs

    def chunk(b, i):
        pos_c = (ncx - 1 - i) if rev else i
        pos_l = (nl - 1 - (i - ncx)) if rev else (i - ncx)
        return jnp.where(i < ncx, n_batch * nl + b * ncx + pos_c, b * nl + pos_l)

    own = lambda col_blk, w: pl.BlockSpec((q, w), lambda b, i: (chunk(b, i), col_blk))
    const = lambda a: pl.BlockSpec(a.shape, lambda b, i: (0, 0))
    in_specs = [own(0, SSD_DIM), own(0, 2 * SSD_BC), own(col_dt // LANES, LANES)]
    args = [xs, bc, p]
    if finish:
        in_specs += [own(COL_Z // SSD_DIM, SSD_DIM), own(0, SSD_DIM)]
        args += [p, y_other]
    tail = [bias, a_vec, expand] + ([d_vec, norm_g] if finish else [])
    in_specs += [const(a) for a in tail]
    args += tail
    return pl.pallas_call(
        functools.partial(_ssd_kernel, rev=rev, finish=finish),
        grid=(n_batch, nl + ncx),
        in_specs=in_specs,
        out_specs=own(0, SSD_DIM),
        out_shape=jax.ShapeDtypeStruct((n, SSD_DIM), BF16 if finish else F32),
        scratch_shapes=[pltpu.VMEM((SSD_STATE, SSD_DIM), F32),
                        pltpu.VMEM((q, SSD_DIM), F32)],
        compiler_params=_params("parallel", "arbitrary"),
        name="ssd_bwd" if rev else "ssd_fwd",
    )(*args)


def _ssd(p, conv_w, conv_b, dt_bias, a_log, d_skip, norm_g, *, n_batch, seq, ctx_len, col_dt):
    cw = jnp.zeros((SUBLANES, SSD_XBC), F32).at[:SSD_CONV].set(conv_w.astype(F32))
    cb = conv_b.astype(F32).reshape(1, SSD_XBC)
    bias = jnp.zeros((1, LANES), F32).at[0, :2 * SSD_HEADS].set(dt_bias.astype(F32).reshape(-1))
    a = -jnp.exp(a_log.astype(F32))
    d_vec = jnp.repeat(d_skip.astype(F32), SSD_HEAD_DIM).reshape(1, SSD_DIM)
    ng = norm_g.astype(F32).reshape(1, SSD_DIM)
    head_of_channel = jnp.arange(SSD_DIM) // SSD_HEAD_DIM
    xs, bc = _ssd_conv(p, cw, cb, n_batch=n_batch, seq=seq, ctx_len=ctx_len)
    outs = {}
    for rev in (True, False):
        lane0 = SSD_HEADS if rev else 0
        a_vec = jnp.zeros((1, LANES), F32).at[0, lane0:lane0 + SSD_HEADS].set(a[1 if rev else 0])
        expand = (jnp.arange(LANES)[:, None] == (head_of_channel[None, :] + lane0)).astype(BF16)
        outs[rev] = _ssd_pass(p, xs, bc, None if rev else outs[True], (bias, a_vec, expand, d_vec, ng),
                              rev=rev, n_batch=n_batch, seq=seq, ctx_len=ctx_len, col_dt=col_dt)
    return outs[False]


def _s5_glu_kernel(y_ref, u_ref, d_ref, w_ref, o_ref):
    y = y_ref[...] + d_ref[...] * u_ref[...]
    g = 0.5 * y * (1.0 + jnp.tanh(np.sqrt(2.0 / np.pi).astype(np.float32) * (y + 0.044715 * (y * y * y))))
    gate = _sigmoid(jnp.dot(w_ref[...], g.astype(BF16), preferred_element_type=F32))
    o_ref[...] = (g * gate).T.astype(BF16)


def _s5_glu(y_t, u_t, d_s5, w_glu_t, *, n_rows):
    tm = ROW_TILE
    cm = pl.BlockSpec((S5_DIM, tm), lambda i: (0, i))
    return pl.pallas_call(
        _s5_glu_kernel,
        grid=(n_rows // tm,),
        in_specs=[cm, cm,
                  pl.BlockSpec((S5_DIM, 1), lambda i: (0, 0)),
                  pl.BlockSpec((S5_DIM, S5_DIM), lambda i: (0, 0))],
        out_specs=pl.BlockSpec((tm, S5_DIM), lambda i: (i, 0)),
        out_shape=jax.ShapeDtypeStruct((n_rows, S5_DIM), BF16),
        compiler_params=_params("parallel"),
        name="s5_glu",
    )(y_t, u_t, d_s5, w_glu_t)


def _merge_kernel(h_ref, mod_ref, oa_ref, os_ref, od_ref, g0_ref, g1_ref, g2_ref,
                  wa_ref, ws_ref, wd_ref, wo_ref, o_ref, acc_ref):
    j = pl.program_id(1)

    @pl.when(j == 0)
    def _():
        acc_ref[...] = jnp.zeros_like(acc_ref)

    m = (_sigmoid(g0_ref[...]) * jnp.dot(oa_ref[...], wa_ref[...], preferred_element_type=F32)
         + _sigmoid(g1_ref[...]) * jnp.dot(os_ref[...], ws_ref[...], preferred_element_type=F32)
         + _sigmoid(g2_ref[...]) * jnp.dot(od_ref[...], wd_ref[...], preferred_element_type=F32))
    acc_ref[...] += jnp.dot(m.astype(BF16), wo_ref[...], preferred_element_type=F32)

    @pl.when(j == pl.num_programs(1) - 1)
    def _():
        o_ref[...] = h_ref[...] + mod_ref[5:6, :] * acc_ref[...]


def _merge(h, mod, p, oa, os5, ossd, w_a, w_s, w_d, w_o, *, n_rows, tiles_per_batch, n_batch):
    d = h.shape[1]
    tm, tc = ROW_TILE, 512
    nj = d // tc
    midx = lambda i, j: (jnp.minimum(i // tiles_per_batch, n_batch), 0, 0)
    row = lambda w: pl.BlockSpec((tm, w), lambda i, j: (i, 0))
    gate = lambda br: pl.BlockSpec((tm, tc), lambda i, j: (i, (COL_GATE + br * d) // tc + j))
    wcol = lambda k: pl.BlockSpec((k, tc), lambda i, j: (0, j))
    return pl.pallas_call(
        _merge_kernel,
        grid=(n_rows // tm, nj),
        in_specs=[row(d), pl.BlockSpec((None, N_MOD, d), midx),
                  row(Q_DIM), row(S5_DIM), row(SSD_DIM), gate(0), gate(1), gate(2),
                  wcol(Q_DIM), wcol(S5_DIM), wcol(SSD_DIM),
                  pl.BlockSpec((tc, d), lambda i, j: (j, 0))],
        out_specs=row(d),
        out_shape=jax.ShapeDtypeStruct((n_rows, d), F32),
        scratch_shapes=[pltpu.VMEM((tm, d), F32)],
        compiler_params=_params("parallel", "arbitrary"),
        name="merge",
    )(h, mod, oa, os5, ossd, p, p, p, w_a, w_s, w_d, w_o)


def _rope_tables(seq, extra_rows):
    rows = seq // GRID_W
    row = jnp.repeat(jnp.arange(rows, dtype=F32), GRID_W)
    col = jnp.tile(jnp.arange(GRID_W, dtype=F32), rows)
    axis_dim = HEAD_DIM // 2
    inv_freq = ROPE_BASE ** (-jnp.arange(0, axis_dim, 2, dtype=F32) / axis_dim)
    ang_r = row[:, None] * inv_freq[None, :]
    ang_c = col[:, None] * inv_freq[None, :]
    cos = jnp.concatenate([jnp.cos(ang_r), jnp.cos(ang_c)] * 2, axis=1)
    sin = jnp.concatenate([-jnp.sin(ang_r), -jnp.sin(ang_c), jnp.sin(ang_r), jnp.sin(ang_c)], axis=1)
    cos = jnp.concatenate([cos, jnp.ones((extra_rows, HEAD_DIM), F32)], axis=0)
    sin = jnp.concatenate([sin, jnp.zeros((extra_rows, HEAD_DIM), F32)], axis=0)
    return cos, sin


def _pair_major(t):
    lead = t.shape[:-1]
    t = t.reshape(lead + (-1, 2, 2, HEAD_DIM // 4))
    return jnp.swapaxes(t, -3, -2).reshape(lead + (-1,))


def _permute_w_in(w, d):
    v0 = Q_DIM + KV_DIM
    s0 = Q_DIM + 2 * KV_DIM
    z0 = s0 + S5_DIM
    x0 = z0 + SSD_DIM
    t0 = x0 + SSD_XBC
    g0 = t0 + 2 * SSD_HEADS
    g1 = g0 + N_BRANCH * d
    parts = [_pair_major(w[:, :v0]), w[:, v0:s0], w[:, x0:t0], w[:, t0:g0],
             jnp.zeros((w.shape[0], DT_SLOT - 2 * SSD_HEADS), w.dtype), w[:, z0:x0], w[:, g0:g1]]
    return jnp.concatenate(parts, axis=1).astype(BF16), w[:, s0:z0].T.astype(BF16)


def kernel(x, c, ctx, c_ctx, w_ada, b_ada, norm_g, w_ffn_gate, w_ffn_up, w_ffn_down, w_in, qk_norm_g, attn_sink,
           w_attn_o, s5_a_re, s5_a_im, s5_log_step, s5_b_re, s5_b_im, s5_c_re, s5_c_im, s5_d, s5_w_glu, w_s5_o,
           ssd_conv_w, ssd_conv_b, ssd_dt_bias, ssd_a_log, ssd_d, ssd_norm_g, w_ssd_o, w_out):
    n_batch, seq, d = x.shape
    ctx_len = ctx.shape[1]
    depth = w_ada.shape[0]
    n_lat, n_ctx = n_batch * seq, n_batch * ctx_len
    n = n_lat + n_ctx
    assert seq % ROW_TILE == 0 and n_ctx % ROW_TILE == 0 and ctx_len % SSD_Q == 0
    assert (COL_GATE + N_BRANCH * d) % PROJ_TN == 0 and n_batch + 1 <= SUBLANES

    h, h_ctx = x.reshape(n_lat, d), ctx.reshape(n_ctx, d)
    c_rows = jnp.zeros((SUBLANES, d), F32).at[:n_batch].set(c.astype(F32)).at[n_batch].set(c_ctx.astype(F32))
    mods = _ada(c_rows, w_ada, b_ada)[:, :n_batch + 1].reshape(depth, n_batch + 1, N_MOD, d)
    cos_t, sin_t = _rope_tables(seq, ROW_TILE)

    tpb = seq // ROW_TILE
    for l in range(depth):
        last = l == depth - 1
        ffn_w = lambda k: (w_ffn_gate[l, k].astype(BF16), w_ffn_up[l, k].astype(BF16), w_ffn_down[l, k].astype(BF16))
        g = norm_g[l].astype(F32)
        h = _ffn(h, mods[l], g, *ffn_w(0), rows=(0, 1, 2, 0), n_rows=n, tiles_per_batch=tpb, n_batch=n_batch,
                 h_ctx=h_ctx if l == 0 else None)
        p, u_t = _proj(h, mods[l], g, *_permute_w_in(w_in[l], d), tiles_per_batch=tpb, n_batch=n_batch)

        qkv = _prep(p, cos_t, sin_t, _pair_major(qk_norm_g[l].astype(F32)), tiles_lat=n_lat // ROW_TILE,
                    tiles_per_batch=tpb)
        oa = _attention(qkv, attn_sink[l].astype(F32), n_batch=n_batch, seq=seq, ctx_len=ctx_len, with_ctx=not last)

        tables = _s5_tables(s5_a_re[l], s5_a_im[l], s5_log_step[l], s5_b_re[l], s5_b_im[l], s5_c_re[l], s5_c_im[l])
        y_t = _s5(u_t, tables, n_batch=n_batch, seq=seq, ctx_len=ctx_len)

        ossd = _ssd(p, ssd_conv_w[l], ssd_conv_b[l], ssd_dt_bias[l], ssd_a_log[l], ssd_d[l], ssd_norm_g[l],
                    n_batch=n_batch, seq=seq, ctx_len=ctx_len, col_dt=COL_DT)

        n_out = n_lat if last else n
        os5 = _s5_glu(y_t, u_t, s5_d[l].astype(F32).reshape(S5_DIM, 1),
                      s5_w_glu[l].T.astype(BF16), n_rows=n_out)
        h = _merge(h, mods[l], p, oa, os5, ossd,
                   w_attn_o[l].astype(BF16), w_s5_o[l].astype(BF16), w_ssd_o[l].astype(BF16), w_out[l].astype(BF16),
                   n_rows=n_out, tiles_per_batch=tpb, n_batch=n_batch)
        h = _ffn(h, mods[l], g, *ffn_w(1), rows=(6, 7, 8, 2), n_rows=n_out, tiles_per_batch=tpb, n_batch=n_batch)
    return h.reshape(n_batch, seq, d)
```

```python
import functools

import numpy as np
import jax
import jax.numpy as jnp
from jax import lax
from jax.experimental import pallas as pl
from jax.experimental.pallas import tpu as pltpu

F32 = jnp.float32
BF16 = jnp.bfloat16
HIGHEST = lax.Precision.HIGHEST

GRID_W = 64
N_MOD = 9
NORM_EPS = 1e-6
ATTN_HEADS = 12
ATTN_KV_HEADS = 4
HEAD_DIM = 128
ATTN_BLOCK = 128
ROPE_BASE = 10000.0
S5_DIM = 1024
S5_GROUP = 16
S5_GROUPS = S5_DIM // S5_GROUP
S5_STATE = 64
SSD_DIM = 1536
SSD_HEAD_DIM = 64
SSD_HEADS = SSD_DIM // SSD_HEAD_DIM
SSD_GROUPS = 4
SSD_STATE = 128
SSD_CONV = 5
N_BRANCH = 3
Q_DIM = ATTN_HEADS * HEAD_DIM
KV_DIM = ATTN_KV_HEADS * HEAD_DIM
SSD_BC = SSD_GROUPS * SSD_STATE
SSD_XBC = SSD_DIM + 2 * SSD_BC

VMEM_LIMIT_BYTES = 56 * 1024 * 1024
LANES = 128
SUBLANES = 8

COL_XBC = Q_DIM + 2 * KV_DIM
COL_DT = COL_XBC + SSD_XBC
DT_SLOT = 1024
COL_Z = COL_DT + DT_SLOT
COL_GATE = COL_Z + SSD_DIM
PROJ_TN = 1536

ROW_TILE = 512
S5_T = 128
SSD_Q = 128


def _params(*sem):
    return pltpu.CompilerParams(dimension_semantics=sem, vmem_limit_bytes=VMEM_LIMIT_BYTES)


def _sigmoid(v):
    return 0.5 * (1.0 + jnp.tanh(0.5 * v))


def _silu(v):
    return v * _sigmoid(v)


def _split_bf16(v):
    hi = v.astype(BF16)
    return hi, (v - hi.astype(F32)).astype(BF16)


def _rms_mod(x, g, shift, scale):
    y = x * lax.rsqrt(jnp.mean(x * x, axis=-1, keepdims=True) + NORM_EPS) * g
    return y * (1.0 + scale) + shift


def _ada_kernel(c_ref, w_ref, b_ref, o_ref):
    s = _silu(c_ref[...]).astype(BF16)
    o_ref[...] = jnp.dot(s, w_ref[...].astype(BF16), preferred_element_type=F32) + b_ref[...]


def _ada(c_rows, w_ada, b_ada):
    depth, d, n = w_ada.shape
    tn = 1024
    return pl.pallas_call(
        _ada_kernel,
        grid=(depth, n // tn),
        in_specs=[pl.BlockSpec((SUBLANES, d), lambda l, j: (0, 0)),
                  pl.BlockSpec((None, d, tn), lambda l, j: (l, 0, j)),
                  pl.BlockSpec((None, 1, tn), lambda l, j: (l, 0, j))],
        out_specs=pl.BlockSpec((None, SUBLANES, tn), lambda l, j: (l, 0, j)),
        out_shape=jax.ShapeDtypeStruct((depth, SUBLANES, n), F32),
        compiler_params=_params("parallel", "parallel"),
        name="ada_mod",
    )(c_rows, w_ada, b_ada.reshape(depth, 1, n))


def _ffn_kernel(*refs, rows, tiles_lat):
    r_shift, r_scale, r_gate, r_norm = rows
    j = pl.program_id(1)
    if tiles_lat is None:
        h_ref, mod_ref, g_ref, wg_ref, wu_ref, wd_ref, o_ref, xn_ref, acc_ref = refs
    else:
        lat_ref, ctx_ref, mod_ref, g_ref, wg_ref, wu_ref, wd_ref, o_ref, xn_ref, acc_ref, h_ref = refs
        is_lat = pl.program_id(0) < tiles_lat

        @pl.when(jnp.logical_and(j == 0, is_lat))
        def _():
            h_ref[...] = lat_ref[...]

        @pl.when(jnp.logical_and(j == 0, jnp.logical_not(is_lat)))
        def _():
            h_ref[...] = ctx_ref[...]

    @pl.when(j == 0)
    def _():
        xn = _rms_mod(h_ref[...], g_ref[r_norm:r_norm + 1, :], mod_ref[r_shift:r_shift + 1, :],
                      mod_ref[r_scale:r_scale + 1, :])
        xn_ref[...] = xn.astype(BF16)
        acc_ref[...] = jnp.zeros_like(acc_ref)

    xn = xn_ref[...]
    a = jnp.dot(xn, wg_ref[...], preferred_element_type=F32)
    b = jnp.dot(xn, wu_ref[...], preferred_element_type=F32)
    t = (_silu(a) * b).astype(BF16)
    acc_ref[...] += jnp.dot(t, wd_ref[...], preferred_element_type=F32)

    @pl.when(j == pl.num_programs(1) - 1)
    def _():
        o_ref[...] = h_ref[...] + 0.5 * mod_ref[r_gate:r_gate + 1, :] * acc_ref[...]


def _ffn(h, mod, g, wg, wu, wd, *, rows, n_rows, tiles_per_batch, n_batch, h_ctx=None):
    d = h.shape[1]
    f = wg.shape[1]
    tm, tf = ROW_TILE, 512
    midx = lambda i, j: (jnp.minimum(i // tiles_per_batch, n_batch), 0, 0)
    scratch = [pltpu.VMEM((tm, d), BF16), pltpu.VMEM((tm, d), F32)]
    if h_ctx is None:
        tiles_lat = None
        srcs, src_specs = [h], [pl.BlockSpec((tm, d), lambda i, j: (i, 0))]
    else:
        tiles_lat = h.shape[0] // tm
        srcs = [h, h_ctx]
        src_specs = [pl.BlockSpec((tm, d), lambda i, j: (jnp.minimum(i, tiles_lat - 1), 0)),
                     pl.BlockSpec((tm, d), lambda i, j: (jnp.maximum(i - tiles_lat, 0), 0))]
        scratch.append(pltpu.VMEM((tm, d), F32))
    return pl.pallas_call(
        functools.partial(_ffn_kernel, rows=rows, tiles_lat=tiles_lat),
        grid=(n_rows // tm, f // tf),
        in_specs=src_specs + [
                  pl.BlockSpec((None, N_MOD, d), midx),
                  pl.BlockSpec(g.shape, lambda i, j: (0, 0)),
                  pl.BlockSpec((d, tf), lambda i, j: (0, j)),
                  pl.BlockSpec((d, tf), lambda i, j: (0, j)),
                  pl.BlockSpec((tf, d), lambda i, j: (j, 0))],
        out_specs=pl.BlockSpec((tm, d), lambda i, j: (i, 0)),
        out_shape=jax.ShapeDtypeStruct((n_rows, d), F32),
        scratch_shapes=scratch,
        compiler_params=_params("parallel", "arbitrary"),
        name="ffn",
    )(*srcs, mod, g, wg, wu, wd)


def _proj_kernel(h_ref, mod_ref, g_ref, w_ref, ws_ref, o_ref, ut_ref, u_ref):
    @pl.when(pl.program_id(1) == 0)
    def _():
        u = _rms_mod(h_ref[...], g_ref[1:2, :], mod_ref[3:4, :], mod_ref[4:5, :]).astype(BF16)
        u_ref[...] = u
        ut_ref[...] = lax.dot_general(ws_ref[...], u, (((1,), (1,)), ((), ())), preferred_element_type=F32)

    o_ref[...] = jnp.dot(u_ref[...], w_ref[...], preferred_element_type=F32)


def _proj(h, mod, g, w, ws_t, *, tiles_per_batch, n_batch):
    n, d = h.shape
    ncol = w.shape[1]
    tm, tn = ROW_TILE, PROJ_TN
    midx = lambda i, j: (jnp.minimum(i // tiles_per_batch, n_batch), 0, 0)
    return pl.pallas_call(
        _proj_kernel,
        grid=(n // tm, ncol // tn),
        in_specs=[pl.BlockSpec((tm, d), lambda i, j: (i, 0)),
                  pl.BlockSpec((None, N_MOD, d), midx),
                  pl.BlockSpec(g.shape, lambda i, j: (0, 0)),
                  pl.BlockSpec((d, tn), lambda i, j: (0, j)),
                  pl.BlockSpec(ws_t.shape, lambda i, j: (0, 0))],
        out_specs=[pl.BlockSpec((tm, tn), lambda i, j: (i, j)),
                   pl.BlockSpec((S5_DIM, tm), lambda i, j: (0, i))],
        out_shape=[jax.ShapeDtypeStruct((n, ncol), F32), jax.ShapeDtypeStruct((S5_DIM, n), F32)],
        scratch_shapes=[pltpu.VMEM((tm, d), BF16)],
        compiler_params=_params("parallel", "arbitrary"),
        name="in_proj",
    )(h, mod, g, w, ws_t)


def _prep_kernel(p_ref, cos_ref, sin_ref, g_ref, o_ref):
    cos = cos_ref[...]
    sin = sin_ref[...]
    scale = HEAD_DIM ** -0.5
    mean_mat = jnp.full((HEAD_DIM, HEAD_DIM), 1.0 / HEAD_DIM, BF16)
    for hd in range(ATTN_HEADS + ATTN_KV_HEADS):
        sl = slice(hd * HEAD_DIM, (hd + 1) * HEAD_DIM)
        x = p_ref[:, sl]
        g = g_ref[0:1, :] if hd < ATTN_HEADS else g_ref[1:2, :]
        sq_hi, sq_lo = _split_bf16(x * x)
        ms = (jnp.dot(sq_hi, mean_mat, preferred_element_type=F32)
              + jnp.dot(sq_lo, mean_mat, preferred_element_type=F32))
        y = x * lax.rsqrt(ms + NORM_EPS) * g
        y = y * cos + pltpu.roll(y, HEAD_DIM // 2, 1) * sin
        if hd < ATTN_HEADS:
            y = y * scale
        o_ref[:, sl] = y.astype(BF16)
    vs = slice(Q_DIM + KV_DIM, Q_DIM + 2 * KV_DIM)
    o_ref[:, vs] = p_ref[:, vs].astype(BF16)


def _prep(p, cos_t, sin_t, qk_g, *, tiles_lat, tiles_per_batch):
    n = p.shape[0]
    tm = ROW_TILE
    w = Q_DIM + 2 * KV_DIM
    tidx = lambda i: (jnp.where(i < tiles_lat, i % tiles_per_batch, tiles_per_batch), 0)
    return pl.pallas_call(
        _prep_kernel,
        grid=(n // tm,),
        in_specs=[pl.BlockSpec((tm, w), lambda i: (i, 0)),
                  pl.BlockSpec((tm, HEAD_DIM), tidx),
                  pl.BlockSpec((tm, HEAD_DIM), tidx),
                  pl.BlockSpec(qk_g.shape, lambda i: (0, 0))],
        out_specs=pl.BlockSpec((tm, w), lambda i: (i, 0)),
        out_shape=jax.ShapeDtypeStruct((n, w), BF16),
        compiler_params=_params("parallel"),
        name="qk_prep",
    )(p, cos_t, sin_t, qk_g)


def _attn_kernel(sink_ref, q_ref, kp_ref, ko_ref, kn_ref, vp_ref, vo_ref, vn_ref, kc_ref, vc_ref, o_ref, *, nb):
    n = pl.program_id(1)
    blk = ATTN_BLOCK
    grp = ATTN_HEADS // ATTN_KV_HEADS
    is_lat = n < nb
    has_prev = jnp.logical_and(is_lat, n > 0)
    has_next = jnp.logical_and(is_lat, n < nb - 1)
    rows = grp * blk
    r = lax.broadcasted_iota(jnp.int32, (rows, 3 * blk), 0)
    kk = lax.broadcasted_iota(jnp.int32, (rows, 3 * blk), 1)
    i = r & (blk - 1)
    far = 4 * blk
    off_prev = jnp.where(has_prev, 0, far)
    off_own = jnp.where(is_lat, 0, far)
    off_next = jnp.where(has_next, 0, far)
    mask = (((kk < blk) & (kk >= i + off_prev))
            | ((kk >= blk + off_own) & (kk < 2 * blk))
            | ((kk >= 2 * blk) & (kk - 2 * blk <= i - off_next)))
    rr = lax.broadcasted_iota(jnp.int32, (rows, 1), 0)
    nt = (((1,), (1,)), ((), ()))
    for kv in range(ATTN_KV_HEADS):
        ks = slice(kv * HEAD_DIM, (kv + 1) * HEAD_DIM)
        qh = jnp.concatenate([q_ref[:, (kv * grp + g) * HEAD_DIM:(kv * grp + g + 1) * HEAD_DIM] for g in range(grp)],
                             axis=0)
        kl = jnp.concatenate([kp_ref[:, ks], ko_ref[:, ks], kn_ref[:, ks]], axis=0)
        vl = jnp.concatenate([vp_ref[:, ks], vo_ref[:, ks], vn_ref[:, ks]], axis=0)
        s_loc = lax.dot_general(qh, kl, nt, preferred_element_type=F32)
        s_ctx = lax.dot_general(qh, kc_ref[:, ks], nt, preferred_element_type=F32)
        s_loc = jnp.where(mask, s_loc, -jnp.inf)
        sink = jnp.full((rows, 1), sink_ref[kv * grp], F32)
        for g in range(1, grp):
            sink = jnp.where(rr >= g * blk, sink_ref[kv * grp + g], sink)
        m = jnp.maximum(jnp.maximum(jnp.max(s_loc, axis=-1, keepdims=True),
                                    jnp.max(s_ctx, axis=-1, keepdims=True)), sink)
        p_loc = jnp.exp(s_loc - m)
        p_ctx = jnp.exp(s_ctx - m)
        denom = (jnp.sum(p_loc, axis=-1, keepdims=True) + jnp.sum(p_ctx, axis=-1, keepdims=True)
                 + jnp.exp(sink - m))
        o = (jnp.dot(p_loc.astype(BF16), vl, preferred_element_type=F32)
             + jnp.dot(p_ctx.astype(BF16), vc_ref[:, ks], preferred_element_type=F32))
        o = o * (1.0 / denom)
        for g in range(grp):
            hs = slice((kv * grp + g) * HEAD_DIM, (kv * grp + g + 1) * HEAD_DIM)
            o_ref[:, hs] = o[g * blk:(g + 1) * blk, :].astype(BF16)


def _attention(qkv, sink, *, n_batch, seq, ctx_len, with_ctx):
    n = qkv.shape[0]
    blk = ATTN_BLOCK
    nb = seq // blk
    ncb = ctx_len // blk
    steps = nb + (ncb if with_ctx else 0)
    ctx0 = n_batch * nb
    kcol = Q_DIM // KV_DIM

    def own(b, s):
        return jnp.where(s < nb, b * nb + s, ctx0 + b * ncb + (s - nb))

    def prev(b, s):
        return jnp.where(s < nb, b * nb + jnp.maximum(s - 1, 0), ctx0 + b * ncb)

    def nxt(b, s):
        return jnp.where(s < nb, b * nb + jnp.minimum(s + 1, nb - 1), ctx0 + b * ncb)

    ctx_blk = lambda b, s: ((n_batch * seq) // ctx_len + b)
    kv_spec = lambda f, col: pl.BlockSpec((blk, KV_DIM), lambda b, s: (f(b, s), col))
    return pl.pallas_call(
        functools.partial(_attn_kernel, nb=nb),
        grid=(n_batch, steps),
        in_specs=[pl.BlockSpec(memory_space=pltpu.SMEM),
                  pl.BlockSpec((blk, Q_DIM), lambda b, s: (own(b, s), 0)),
                  kv_spec(prev, kcol), kv_spec(own, kcol), kv_spec(nxt, kcol),
                  kv_spec(prev, kcol + 1), kv_spec(own, kcol + 1), kv_spec(nxt, kcol + 1),
                  pl.BlockSpec((ctx_len, KV_DIM), lambda b, s: (ctx_blk(b, s), kcol)),
                  pl.BlockSpec((ctx_len, KV_DIM), lambda b, s: (ctx_blk(b, s), kcol + 1))],
        out_specs=pl.BlockSpec((blk, Q_DIM), lambda b, s: (own(b, s), 0)),
        out_shape=jax.ShapeDtypeStruct((n if with_ctx else n_batch * seq, Q_DIM), BF16),
        compiler_params=_params("parallel", "parallel"),
        name="window_attn",
    )(sink, qkv, qkv, qkv, qkv, qkv, qkv, qkv, qkv, qkv)


def _s5_tables(a_re, a_im, log_step, b_re, b_im, c_re, c_im):
    t = S5_T
    step = jnp.exp(log_step.astype(F32))[..., None]
    a_re = a_re.astype(F32)
    a_im = a_im.astype(F32)
    k = jnp.arange(t + 1, dtype=F32)[:, None, None, None]
    mag = jnp.exp(k * (a_re * step)[None])
    ang = k * (a_im * step)[None]
    pr, pi = mag * jnp.cos(ang), mag * jnp.sin(ang)
    den = a_re * a_re + a_im * a_im
    nr, ni = pr[1] - 1.0, pi[1]
    wr = (nr * a_re + ni * a_im) / den
    wi = (ni * a_re - nr * a_im) / den
    b_re = b_re.astype(F32)[None]
    b_im = b_im.astype(F32)[None]
    wbr = wr[..., None] * b_re - wi[..., None] * b_im
    wbi = wr[..., None] * b_im + wi[..., None] * b_re
    c_re = c_re.astype(F32)
    c_im = c_im.astype(F32)
    grp = S5_GROUP

    wbr_t = jnp.swapaxes(wbr, 2, 3)[:, :, :, None, :]
    wbi_t = jnp.swapaxes(wbi, 2, 3)[:, :, :, None, :]
    qr = (c_re[None, :, None] * wbr_t - c_im[None, :, None] * wbi_t).reshape(2, S5_GROUPS, grp * grp, S5_STATE)
    qi = (c_re[None, :, None] * wbi_t + c_im[None, :, None] * wbr_t).reshape(2, S5_GROUPS, grp * grp, S5_STATE)

    def taps(d, pw_r, pw_i):
        return (jnp.einsum('gmp,kgp->gmk', qr[d], pw_r, precision=HIGHEST)
                - jnp.einsum('gmp,kgp->gmk', qi[d], pw_i, precision=HIGHEST))

    kf = taps(0, pr[:t, 0], pi[:t, 0])
    kf = kf.at[:, :, 0].add(jnp.sum(qr[1], axis=-1))
    kb = taps(1, pr[t:0:-1, 1], pi[t:0:-1, 1])

    pw_f = jnp.concatenate([pr[t - 1::-1, 0], pr[:t, 1], pi[t - 1::-1, 0], pi[:t, 1]], axis=-1)
    pw_f = jnp.transpose(pw_f, (1, 0, 2))
    wb = jnp.concatenate([wbr[0], wbr[1], wbi[0], wbi[1]], axis=1)
    wb = jnp.swapaxes(wb, 1, 2)
    pw_e = jnp.concatenate([pr[1:t + 1, 0], pr[t:0:-1, 1], pi[1:t + 1, 0], pi[t:0:-1, 1]], axis=-1)
    pw_e = jnp.transpose(pw_e, (1, 2, 0))
    c_tab = jnp.concatenate([jnp.swapaxes(c_re, 1, 2), jnp.swapaxes(c_im, 1, 2)], axis=2)
    c_tab = jnp.concatenate([c_tab, c_tab], axis=1)
    dec_re = jnp.concatenate([pr[t, 0], pr[t, 1]], axis=-1)
    dec_im = jnp.concatenate([pi[t, 0], pi[t, 1]], axis=-1)
    return kf, kb, pw_f, wb, pw_e, c_tab, dec_re, dec_im


def _s5_chunk_rows(u_ref):
    nc = u_ref.shape[1] // S5_T
    x = pltpu.einshape("icl->cil", u_ref[...].reshape(S5_GROUP, nc, S5_T))
    return x.reshape(nc, S5_GROUP * S5_T).astype(BF16)


def _s5_state_kernel(u_ref, pw_ref, wb_ref, sre_ref, sim_ref, f_ref):
    half = 2 * S5_STATE
    pw_re, pw_im = pw_ref[:, :half], pw_ref[:, half:]
    for i in range(S5_GROUP):
        wb_re, wb_im = wb_ref[i:i + 1, :half], wb_ref[i:i + 1, half:]
        blk = jnp.concatenate([pw_re * wb_re - pw_im * wb_im, pw_re * wb_im + pw_im * wb_re], axis=1)
        f_ref[i * S5_T:(i + 1) * S5_T, :] = blk.astype(BF16)
    s = jnp.dot(_s5_chunk_rows(u_ref), f_ref[...], preferred_element_type=F32)
    sre_ref[...] = s[:, :2 * S5_STATE]
    sim_ref[...] = s[:, 2 * S5_STATE:]


def _s5_rec_kernel(sre_ref, sim_ref, are_ref, aim_ref, hre_ref, him_ref, *, n_batch, nlat, nctx):
    are = are_ref[...]
    aim = aim_ref[...]
    fwd = lax.broadcasted_iota(jnp.int32, are.shape, 1) < S5_STATE
    half = S5_STATE

    def body(i, carry):
        out = []
        for b in range(n_batch):
            hr, hi = carry[b]
            lat0 = b * nlat
            ctx0 = n_batch * nlat + b * nctx
            in_ctx = i < nctx
            fi = jnp.where(in_ctx, ctx0 + i, lat0 + i - nctx)
            bi = jnp.where(in_ctx, ctx0 + nctx - 1 - i, lat0 + nlat - 1 - (i - nctx))
            hre_ref[fi, :, 0:half] = hr[:, 0:half]
            hre_ref[bi, :, half:] = hr[:, half:]
            him_ref[fi, :, 0:half] = hi[:, 0:half]
            him_ref[bi, :, half:] = hi[:, half:]
            sr = jnp.where(fwd, sre_ref[fi], sre_ref[bi])
            si = jnp.where(fwd, sim_ref[fi], sim_ref[bi])
            out.append((are * hr - aim * hi + sr, are * hi + aim * hr + si))
        return tuple(out)

    zero = jnp.zeros(are.shape, F32)
    lax.fori_loop(0, nlat + nctx, body, tuple((zero, zero) for _ in range(n_batch)))


def _s5_out_kernel(u_ref, hre_ref, him_ref, kf_ref, kb_ref, pw_ref, c_ref, y_ref, m_ref, e_ref):
    t = S5_T
    half = 2 * S5_STATE
    pw_re, pw_im = pw_ref[:half, :], pw_ref[half:, :]
    for o in range(S5_GROUP):
        c_re = c_ref[:, o:o + 1]
        c_im = c_ref[:, S5_GROUP + o:S5_GROUP + o + 1]
        blk = jnp.concatenate([c_re * pw_re - c_im * pw_im, -(c_re * pw_im + c_im * pw_re)], axis=0)
        e_ref[:, o * t:(o + 1) * t] = blk.astype(BF16)
    s_idx = lax.broadcasted_iota(jnp.int32, (t, t), 0)
    t_idx = lax.broadcasted_iota(jnp.int32, (t, t), 1)
    causal = t_idx >= s_idx

    def fill(i, carry):
        r0 = pl.multiple_of(i * S5_GROUP, S5_GROUP)
        kf = kf_ref[pl.ds(r0, S5_GROUP), :]
        kb = kb_ref[pl.ds(r0, S5_GROUP), :]
        row0 = pl.multiple_of(i * t, t)
        for o in range(S5_GROUP):
            fwd = pltpu.roll(jnp.broadcast_to(kf[o:o + 1, :], (t, t)), 0, 1, stride=1, stride_axis=0)
            bwd = pltpu.roll(jnp.broadcast_to(kb[o:o + 1, :], (t, t)), 0, 1, stride=1, stride_axis=0)
            m_ref[pl.ds(row0, t), o * t:(o + 1) * t] = jnp.where(causal, fwd, bwd).astype(BF16)
        return carry

    lax.fori_loop(0, S5_GROUP, fill, 0)
    y = jnp.dot(_s5_chunk_rows(u_ref), m_ref[...], preferred_element_type=F32)
    hin = jnp.concatenate([hre_ref[...], him_ref[...]], axis=1).astype(BF16)
    y = y + jnp.dot(hin, e_ref[...], preferred_element_type=F32)
    nc = y.shape[0]
    y = pltpu.einshape("cil->icl", y.reshape(nc, S5_GROUP, t))
    y_ref[...] = y.reshape(S5_GROUP, nc * t)


def _s5(u_t, tables, *, n_batch, seq, ctx_len):
    kf, kb, pw_f, wb, pw_e, c_tab, dec_re, dec_im = tables
    g = S5_GROUPS
    n = u_t.shape[1]
    nc = n // S5_T
    tw = S5_GROUP * S5_T
    sw = 2 * S5_STATE
    gspec = lambda shape: pl.BlockSpec((None,) + shape, lambda i: (i, 0, 0))
    uspec = pl.BlockSpec((S5_GROUP, n), lambda i: (i, 0))
    s_re, s_im = pl.pallas_call(
        _s5_state_kernel,
        grid=(g,),
        in_specs=[uspec, gspec(pw_f.shape[1:]), gspec(wb.shape[1:])],
        out_specs=[pl.BlockSpec((nc, sw), lambda i: (0, i)), pl.BlockSpec((nc, sw), lambda i: (0, i))],
        out_shape=[jax.ShapeDtypeStruct((nc, g * sw), F32)] * 2,
        scratch_shapes=[pltpu.VMEM((tw, 2 * sw), BF16)],
        compiler_params=_params("parallel"),
        name="s5_chunk_state",
    )(u_t, pw_f, wb)
    gb = SUBLANES
    blk3 = pl.BlockSpec((nc, gb, sw), lambda i: (0, i, 0))
    dspec = pl.BlockSpec((gb, sw), lambda i: (i, 0))
    h_re, h_im = pl.pallas_call(
        functools.partial(_s5_rec_kernel, n_batch=n_batch, nlat=seq // S5_T, nctx=ctx_len // S5_T),
        grid=(g // gb,),
        in_specs=[blk3, blk3, dspec, dspec],
        out_specs=[blk3, blk3],
        out_shape=[jax.ShapeDtypeStruct((nc, g, sw), F32)] * 2,
        compiler_params=_params("parallel"),
        name="s5_chunk_scan",
    )(s_re.reshape(nc, g, sw), s_im.reshape(nc, g, sw), dec_re, dec_im)
    hspec = pl.BlockSpec((nc, sw), lambda i: (0, i))
    tap = gspec((S5_GROUP * S5_GROUP, S5_T))
    return pl.pallas_call(
        _s5_out_kernel,
        grid=(g,),
        in_specs=[uspec, hspec, hspec, tap, tap, gspec(pw_e.shape[1:]), gspec(c_tab.shape[1:])],
        out_specs=uspec,
        out_shape=jax.ShapeDtypeStruct(u_t.shape, F32),
        scratch_shapes=[pltpu.VMEM((tw, tw), BF16), pltpu.VMEM((2 * sw, tw), BF16)],
        compiler_params=_params("parallel"),
        name="s5_out",
    )(u_t, h_re.reshape(nc, g * sw), h_im.reshape(nc, g * sw), kf, kb, pw_e, c_tab)


def _ssd_conv_kernel(xo_ref, xp_ref, xn_ref, cw_ref, cb_ref, xs_ref, bc_ref, xe_ref, *, n_batch, n_lat_chunks,
                     n_ctx_chunks):
    q = SSD_Q
    halo = SUBLANES
    c = pl.program_id(0)
    lat_total = n_batch * n_lat_chunks
    is_lat = c < lat_total
    pos = jnp.where(is_lat, lax.rem(c, n_lat_chunks), lax.rem(jnp.maximum(c - lat_total, 0), n_ctx_chunks))
    length = jnp.where(is_lat, n_lat_chunks, n_ctx_chunks)
    has_prev = pos > 0
    has_next = pos < length - 1
    xe_ref[0:halo, :] = xp_ref[...] * has_prev.astype(F32)
    xe_ref[halo:halo + q, :] = xo_ref[...]
    xe_ref[halo + q:, :] = xn_ref[...] * has_next.astype(F32)
    xe = xe_ref[...]
    mid = SSD_CONV // 2
    acc = cb_ref[...] + cw_ref[mid:mid + 1, :] * xo_ref[...]
    for k in range(SSD_CONV):
        if k != mid:
            acc = acc + cw_ref[k:k + 1, :] * pltpu.roll(xe, (mid - k) % (q + 2 * halo), 0)[halo:halo + q]
    xbc = _silu(acc)
    xs_ref[...] = xbc[:, :SSD_DIM]
    bc_ref[...] = xbc[:, SSD_DIM:].astype(BF16)


def _ssd_conv(p, conv_w, conv_b, *, n_batch, seq, ctx_len):
    n = p.shape[0]
    q = SSD_Q
    halo = SUBLANES
    hpq = q // halo
    last_halo = n // halo - 1
    col = COL_XBC // SSD_XBC
    return pl.pallas_call(
        functools.partial(_ssd_conv_kernel, n_batch=n_batch, n_lat_chunks=seq // q, n_ctx_chunks=ctx_len // q),
        grid=(n // q,),
        in_specs=[pl.BlockSpec((q, SSD_XBC), lambda c: (c, col)),
                  pl.BlockSpec((halo, SSD_XBC), lambda c: (jnp.maximum(c * hpq - 1, 0), col)),
                  pl.BlockSpec((halo, SSD_XBC), lambda c: (jnp.minimum((c + 1) * hpq, last_halo), col)),
                  pl.BlockSpec(conv_w.shape, lambda c: (0, 0)),
                  pl.BlockSpec(conv_b.shape, lambda c: (0, 0))],
        out_specs=[pl.BlockSpec((q, SSD_DIM), lambda c: (c, 0)), pl.BlockSpec((q, 2 * SSD_BC), lambda c: (c, 0))],
        out_shape=[jax.ShapeDtypeStruct((n, SSD_DIM), F32), jax.ShapeDtypeStruct((n, 2 * SSD_BC), BF16)],
        scratch_shapes=[pltpu.VMEM((q + 2 * halo, SSD_XBC), F32)],
        compiler_params=_params("parallel"),
        name="ssd_conv",
    )(p, p, p, conv_w, conv_b)


def _ssd_kernel(*refs, rev, finish):
    if finish:
        (xs_ref, bc_ref, dt_ref, z_ref, yb_ref, bias_ref, a_ref, ex_ref, d_ref, ng_ref,
         o_ref, h_ref, yd_ref) = refs
    else:
        xs_ref, bc_ref, dt_ref, bias_ref, a_ref, ex_ref, o_ref, h_ref, yd_ref = refs
    q = SSD_Q

    @pl.when(pl.program_id(1) == 0)
    def _():
        h_ref[...] = jnp.zeros_like(h_ref)

    xs = xs_ref[...]
    bm = bc_ref[:, :SSD_BC]
    cm = bc_ref[:, SSD_BC:]

    raw = dt_ref[...] + bias_ref[...]
    dt = jnp.maximum(raw, 0.0) + jnp.log1p(jnp.exp(-jnp.abs(raw)))
    ad = dt * a_ref[...]
    li = lax.broadcasted_iota(jnp.int32, (q, q), 0)
    si = lax.broadcasted_iota(jnp.int32, (q, q), 1)
    keep = (si >= li) if rev else (si <= li)
    tri = keep.astype(F32)
    a_cs = jnp.dot(tri, ad, precision=HIGHEST, preferred_element_type=F32)
    a_cs_t = a_cs.T
    dt_t = dt.T
    edge = 0 if rev else q - 1
    tot = a_cs[edge:edge + 1, :]
    scal = jnp.concatenate([dt * jnp.exp(tot - a_cs), jnp.exp(a_cs)], axis=0)
    s_hi, s_lo = _split_bf16(scal)
    ex = ex_ref[...]
    scal_x = (jnp.dot(s_hi, ex, preferred_element_type=F32) + jnp.dot(s_lo, ex, preferred_element_type=F32))
    ds_x = scal_x[:q]
    do_x = scal_x[q:]
    tot_x = do_x[edge:edge + 1, :]

    xs_b = xs.astype(BF16)
    xds_b = (xs * ds_x).astype(BF16)
    rep = SSD_HEADS // SSD_GROUPS
    gw = rep * SSD_HEAD_DIM
    lane0 = SSD_HEADS if rev else 0
    nt = (((1,), (1,)), ((), ()))
    y_parts = []
    for g in range(SSD_GROUPS):
        bg = bm[:, g * SSD_STATE:(g + 1) * SSD_STATE]
        cg = cm[:, g * SSD_STATE:(g + 1) * SSD_STATE]
        cb = lax.dot_general(cg, bg, nt, preferred_element_type=F32)
        for r in range(rep):
            hd = g * rep + r
            col = a_cs[:, lane0 + hd:lane0 + hd + 1]
            row = a_cs_t[lane0 + hd:lane0 + hd + 1, :]
            dec = jnp.exp(jnp.where(keep, col - row, -jnp.inf)) * dt_t[lane0 + hd:lane0 + hd + 1, :]
            yd_ref[:, hd * SSD_HEAD_DIM:(hd + 1) * SSD_HEAD_DIM] = jnp.dot(
                (cb * dec).astype(BF16), xs_b[:, hd * SSD_HEAD_DIM:(hd + 1) * SSD_HEAD_DIM],
                preferred_element_type=F32)
        gs = slice(g * gw, (g + 1) * gw)
        h_g = h_ref[:, gs]
        y_off = jnp.dot(cg, h_g.astype(BF16), preferred_element_type=F32) * do_x[:, gs]
        bg_t = bg.astype(F32).T.astype(BF16)
        st = jnp.dot(bg_t, xds_b[:, gs], preferred_element_type=F32)
        h_ref[:, gs] = h_g * tot_x[:, gs] + st
        y_parts.append(y_off)
    y = yd_ref[...] + jnp.concatenate(y_parts, axis=1)

    if finish:
        y = y + yb_ref[...] + d_ref[...] * xs
        y = y * _silu(z_ref[...])
        nw = SSD_DIM // SSD_GROUPS
        outs = []
        for g in range(SSD_GROUPS):
            yg = y[:, g * nw:(g + 1) * nw]
            outs.append(yg * lax.rsqrt(jnp.mean(yg * yg, axis=-1, keepdims=True) + NORM_EPS))
        o_ref[...] = (jnp.concatenate(outs, axis=1) * ng_ref[...]).astype(BF16)
    else:
        o_ref[...] = y


def _ssd_pass(p, xs, bc, y_other, vecs, *, rev, n_batch, seq, ctx_len, col_dt):
    n = p.shape[0]
    q = SSD_Q
    nl, ncx = seq // q, ctx_len // q
    finish = y_other is not None
    bias, a_vec, expand, d_vec, norm_g = vecs

    def chunk(b, i):
        pos_c = (ncx - 1 - i) if rev else i
        pos_l = (nl - 1 - (i - ncx)) if rev else (i - ncx)
        return jnp.where(i < ncx, n_batch * nl + b * ncx + pos_c, b * nl + pos_l)

    own = lambda col_blk, w: pl.BlockSpec((q, w), lambda b, i: (chunk(b, i), col_blk))
    const = lambda a: pl.BlockSpec(a.shape, lambda b, i: (0, 0))
    in_specs = [own(0, SSD_DIM), own(0, 2 * SSD_BC), own(col_dt // LANES, LANES)]
    args = [xs, bc, p]
    if finish:
        in_specs += [own(COL_Z // SSD_DIM, SSD_DIM), own(0, SSD_DIM)]
        args += [p, y_other]
    tail = [bias, a_vec, expand] + ([d_vec, norm_g] if finish else [])
    in_specs += [const(a) for a in tail]
    args += tail
    return pl.pallas_call(
        functools.partial(_ssd_kernel, rev=rev, finish=finish),
        grid=(n_batch, nl + ncx),
        in_specs=in_specs,
        out_specs=own(0, SSD_DIM),
        out_shape=jax.ShapeDtypeStruct((n, SSD_DIM), BF16 if finish else F32),
        scratch_shapes=[pltpu.VMEM((SSD_STATE, SSD_DIM), F32),
                        pltpu.VMEM((q, SSD_DIM), F32)],
        compiler_params=_params("parallel", "arbitrary"),
        name="ssd_bwd" if rev else "ssd_fwd",
    )(*args)


def _ssd(p, conv_w, conv_b, dt_bias, a_log, d_skip, norm_g, *, n_batch, seq, ctx_len, col_dt):
    cw = jnp.zeros((SUBLANES, SSD_XBC), F32).at[:SSD_CONV].set(conv_w.astype(F32))
    cb = conv_b.astype(F32).reshape(1, SSD_XBC)
    bias = jnp.zeros((1, LANES), F32).at[0, :2 * SSD_HEADS].set(dt_bias.astype(F32).reshape(-1))
    a = -jnp.exp(a_log.astype(F32))
    d_vec = jnp.repeat(d_skip.astype(F32), SSD_HEAD_DIM).reshape(1, SSD_DIM)
    ng = norm_g.astype(F32).reshape(1, SSD_DIM)
    head_of_channel = jnp.arange(SSD_DIM) // SSD_HEAD_DIM
    xs, bc = _ssd_conv(p, cw, cb, n_batch=n_batch, seq=seq, ctx_len=ctx_len)
    outs = {}
    for rev in (True, False):
        lane0 = SSD_HEADS if rev else 0
        a_vec = jnp.zeros((1, LANES), F32).at[0, lane0:lane0 + SSD_HEADS].set(a[1 if rev else 0])
        expand = (jnp.arange(LANES)[:, None] == (head_of_channel[None, :] + lane0)).astype(BF16)
        outs[rev] = _ssd_pass(p, xs, bc, None if rev else outs[True], (bias, a_vec, expand, d_vec, ng),
                              rev=rev, n_batch=n_batch, seq=seq, ctx_len=ctx_len, col_dt=col_dt)
    return outs[False]


def _s5_glu_kernel(y_ref, u_ref, d_ref, w_ref, o_ref):
    y = y_ref[...] + d_ref[...] * u_ref[...]
    g = 0.5 * y * (1.0 + jnp.tanh(np.sqrt(2.0 / np.pi).astype(np.float32) * (y + 0.044715 * (y * y * y))))
    gate = _sigmoid(jnp.dot(w_ref[...], g.astype(BF16), preferred_element_type=F32))
    o_ref[...] = (g * gate).T.astype(BF16)


def _s5_glu(y_t, u_t, d_s5, w_glu_t, *, n_rows):
    tm = ROW_TILE
    cm = pl.BlockSpec((S5_DIM, tm), lambda i: (0, i))
    return pl.pallas_call(
        _s5_glu_kernel,
        grid=(n_rows // tm,),
        in_specs=[cm, cm,
                  pl.BlockSpec((S5_DIM, 1), lambda i: (0, 0)),
                  pl.BlockSpec((S5_DIM, S5_DIM), lambda i: (0, 0))],
        out_specs=pl.BlockSpec((tm, S5_DIM), lambda i: (i, 0)),
        out_shape=jax.ShapeDtypeStruct((n_rows, S5_DIM), BF16),
        compiler_params=_params("parallel"),
        name="s5_glu",
    )(y_t, u_t, d_s5, w_glu_t)


def _merge_kernel(h_ref, mod_ref, oa_ref, os_ref, od_ref, g0_ref, g1_ref, g2_ref,
                  wa_ref, ws_ref, wd_ref, wo_ref, o_ref, acc_ref):
    j = pl.program_id(1)

    @pl.when(j == 0)
    def _():
        acc_ref[...] = jnp.zeros_like(acc_ref)

    m = (_sigmoid(g0_ref[...]) * jnp.dot(oa_ref[...], wa_ref[...], preferred_element_type=F32)
         + _sigmoid(g1_ref[...]) * jnp.dot(os_ref[...], ws_ref[...], preferred_element_type=F32)
         + _sigmoid(g2_ref[...]) * jnp.dot(od_ref[...], wd_ref[...], preferred_element_type=F32))
    acc_ref[...] += jnp.dot(m.astype(BF16), wo_ref[...], preferred_element_type=F32)

    @pl.when(j == pl.num_programs(1) - 1)
    def _():
        o_ref[...] = h_ref[...] + mod_ref[5:6, :] * acc_ref[...]


def _merge(h, mod, p, oa, os5, ossd, w_a, w_s, w_d, w_o, *, n_rows, tiles_per_batch, n_batch):
    d = h.shape[1]
    tm, tc = ROW_TILE, 512
    nj = d // tc
    midx = lambda i, j: (jnp.minimum(i // tiles_per_batch, n_batch), 0, 0)
    row = lambda w: pl.BlockSpec((tm, w), lambda i, j: (i, 0))
    gate = lambda br: pl.BlockSpec((tm, tc), lambda i, j: (i, (COL_GATE + br * d) // tc + j))
    wcol = lambda k: pl.BlockSpec((k, tc), lambda i, j: (0, j))
    return pl.pallas_call(
        _merge_kernel,
        grid=(n_rows // tm, nj),
        in_specs=[row(d), pl.BlockSpec((None, N_MOD, d), midx),
                  row(Q_DIM), row(S5_DIM), row(SSD_DIM), gate(0), gate(1), gate(2),
                  wcol(Q_DIM), wcol(S5_DIM), wcol(SSD_DIM),
                  pl.BlockSpec((tc, d), lambda i, j: (j, 0))],
        out_specs=row(d),
        out_shape=jax.ShapeDtypeStruct((n_rows, d), F32),
        scratch_shapes=[pltpu.VMEM((tm, d), F32)],
        compiler_params=_params("parallel", "arbitrary"),
        name="merge",
    )(h, mod, oa, os5, ossd, p, p, p, w_a, w_s, w_d, w_o)


def _rope_tables(seq, extra_rows):
    rows = seq // GRID_W
    row = jnp.repeat(jnp.arange(rows, dtype=F32), GRID_W)
    col = jnp.tile(jnp.arange(GRID_W, dtype=F32), rows)
    axis_dim = HEAD_DIM // 2
    inv_freq = ROPE_BASE ** (-jnp.arange(0, axis_dim, 2, dtype=F32) / axis_dim)
    ang_r = row[:, None] * inv_freq[None, :]
    ang_c = col[:, None] * inv_freq[None, :]
    cos = jnp.concatenate([jnp.cos(ang_r), jnp.cos(ang_c)] * 2, axis=1)
    sin = jnp.concatenate([-jnp.sin(ang_r), -jnp.sin(ang_c), jnp.sin(ang_r), jnp.sin(ang_c)], axis=1)
    cos = jnp.concatenate([cos, jnp.ones((extra_rows, HEAD_DIM), F32)], axis=0)
    sin = jnp.concatenate([sin, jnp.zeros((extra_rows, HEAD_DIM), F32)], axis=0)
    return cos, sin


def _pair_major(t):
    lead = t.shape[:-1]
    t = t.reshape(lead + (-1, 2, 2, HEAD_DIM // 4))
    return jnp.swapaxes(t, -3, -2).reshape(lead + (-1,))


def _permute_w_in(w, d):
    v0 = Q_DIM + KV_DIM
    s0 = Q_DIM + 2 * KV_DIM
    z0 = s0 + S5_DIM
    x0 = z0 + SSD_DIM
    t0 = x0 + SSD_XBC
    g0 = t0 + 2 * SSD_HEADS
    g1 = g0 + N_BRANCH * d
    parts = [_pair_major(w[:, :v0]), w[:, v0:s0], w[:, x0:t0], w[:, t0:g0],
             jnp.zeros((w.shape[0], DT_SLOT - 2 * SSD_HEADS), w.dtype), w[:, z0:x0], w[:, g0:g1]]
    return jnp.concatenate(parts, axis=1).astype(BF16), w[:, s0:z0].T.astype(BF16)


def kernel(x, c, ctx, c_ctx, w_ada, b_ada, norm_g, w_ffn_gate, w_ffn_up, w_ffn_down, w_in, qk_norm_g, attn_sink,
           w_attn_o, s5_a_re, s5_a_im, s5_log_step, s5_b_re, s5_b_im, s5_c_re, s5_c_im, s5_d, s5_w_glu, w_s5_o,
           ssd_conv_w, ssd_conv_b, ssd_dt_bias, ssd_a_log, ssd_d, ssd_norm_g, w_ssd_o, w_out):
    n_batch, seq, d = x.shape
    ctx_len = ctx.shape[1]
    depth = w_ada.shape[0]
    n_lat, n_ctx = n_batch * seq, n_batch * ctx_len
    n = n_lat + n_ctx
    assert seq % ROW_TILE == 0 and n_ctx % ROW_TILE == 0 and ctx_len % SSD_Q == 0
    assert (COL_GATE + N_BRANCH * d) % PROJ_TN == 0 and n_batch + 1 <= SUBLANES

    h, h_ctx = x.reshape(n_lat, d), ctx.reshape(n_ctx, d)
    c_rows = jnp.zeros((SUBLANES, d), F32).at[:n_batch].set(c.astype(F32)).at[n_batch].set(c_ctx.astype(F32))
    mods = _ada(c_rows, w_ada, b_ada)[:, :n_batch + 1].reshape(depth, n_batch + 1, N_MOD, d)
    cos_t, sin_t = _rope_tables(seq, ROW_TILE)

    tpb = seq // ROW_TILE
    for l in range(depth):
        last = l == depth - 1
        ffn_w = lambda k: (w_ffn_gate[l, k].astype(BF16), w_ffn_up[l, k].astype(BF16), w_ffn_down[l, k].astype(BF16))
        g = norm_g[l].astype(F32)
        h = _ffn(h, mods[l], g, *ffn_w(0), rows=(0, 1, 2, 0), n_rows=n, tiles_per_batch=tpb, n_batch=n_batch,
                 h_ctx=h_ctx if l == 0 else None)
        p, u_t = _proj(h, mods[l], g, *_permute_w_in(w_in[l], d), tiles_per_batch=tpb, n_batch=n_batch)

        qkv = _prep(p, cos_t, sin_t, _pair_major(qk_norm_g[l].astype(F32)), tiles_lat=n_lat // ROW_TILE,
                    tiles_per_batch=tpb)
        oa = _attention(qkv, attn_sink[l].astype(F32), n_batch=n_batch, seq=seq, ctx_len=ctx_len, with_ctx=not last)

        tables = _s5_tables(s5_a_re[l], s5_a_im[l], s5_log_step[l], s5_b_re[l], s5_b_im[l], s5_c_re[l], s5_c_im[l])
        y_t = _s5(u_t, tables, n_batch=n_batch, seq=seq, ctx_len=ctx_len)

        ossd = _ssd(p, ssd_conv_w[l], ssd_conv_b[l], ssd_dt_bias[l], ssd_a_log[l], ssd_d[l], ssd_norm_g[l],
                    n_batch=n_batch, seq=seq, ctx_len=ctx_len, col_dt=COL_DT)

        n_out = n_lat if last else n
        os5 = _s5_glu(y_t, u_t, s5_d[l].astype(F32).reshape(S5_DIM, 1),
                      s5_w_glu[l].T.astype(BF16), n_rows=n_out)
        h = _merge(h, mods[l], p, oa, os5, ossd,
                   w_attn_o[l].astype(BF16), w_s5_o[l].astype(BF16), w_ssd_o[l].astype(BF16), w_out[l].astype(BF16),
                   n_rows=n_out, tiles_per_batch=tpb, n_batch=n_batch)
        h = _ffn(h, mods[l], g, *ffn_w(1), rows=(6, 7, 8, 2), n_rows=n_out, tiles_per_batch=tpb, n_batch=n_batch)
    return h.reshape(n_batch, seq, d)
```
